```python
import jax
import jax.numpy as jnp
from jax import lax
import numpy as np

D_MODEL = 2048
BATCH = 2
SEQ = 4096
DEPTH = 4
DEC_BATCH = 8
DEC_SEQ = 8
PAST_LEN = 16384
PAGE_SIZE = 128

N_HEADS = 16
HEAD_DIM = 128
N_KV_HEADS = 4
GROUP = N_HEADS // N_KV_HEADS
CMP_BLOCK = 32
CMP_STRIDE = 16
SEL_BLOCK = 64
TOP_N = 16
N_LOCAL_FORCED = 2
WINDOW = 512
Q_BLOCK = 64
SCALE = HEAD_DIM ** -0.5
ATTN_IN = N_HEADS * HEAD_DIM + 3 * 2 * N_KV_HEADS * HEAD_DIM + 3 * N_HEADS
CONV_W = 3
D_CONV = D_MODEL
D_FF = ((8 * D_MODEL // 3 + 255) // 256) * 256
N_ATTN_LAYERS = (DEPTH + 1) // 2
N_CONV_LAYERS = DEPTH // 2
EPS = 1e-6

kernel_name = 'nsa_shortconv_hybrid_step'


def rmsnorm(x, g):
    xf = x.astype(jnp.float32)
    y = xf * lax.rsqrt(jnp.mean(xf * xf, axis=-1, keepdims=True) + EPS)
    return (y * g.astype(jnp.float32)).astype(x.dtype)


def masked_softmax(s, mask):
    s = jnp.where(mask, s.astype(jnp.float32), -jnp.inf)
    m = jnp.max(s, axis=-1, keepdims=True)
    e = jnp.exp(s - jnp.where(jnp.isfinite(m), m, 0.0))
    return e / jnp.maximum(jnp.sum(e, axis=-1, keepdims=True), 1e-30)


def nsa_project(h, w_in):
    bsz, t = h.shape[:2]
    p = jnp.einsum('btd,de->bte', h, w_in)
    n_q = N_HEADS * HEAD_DIM
    n_kv = 3 * 2 * N_KV_HEADS * HEAD_DIM
    q = p[..., :n_q].reshape(bsz, t, N_KV_HEADS, GROUP, HEAD_DIM).transpose(0, 2, 3, 1, 4)
    kv = p[..., n_q:n_q + n_kv].reshape(bsz, t, 3, N_KV_HEADS, 2, HEAD_DIM)
    gates = jax.nn.sigmoid(p[..., n_q + n_kv:].astype(jnp.float32)).astype(p.dtype)
    gates = gates.reshape(bsz, t, 3, N_KV_HEADS, GROUP)
    return q, kv[:, :, 0], kv[:, :, 1], kv[:, :, 2], gates


def compress(kv, pool_w, pe, w1, w2):
    bsz, n_rows = kv.shape[:2]
    n_half = CMP_BLOCK // CMP_STRIDE
    n_chunks = n_rows // CMP_STRIDE
    n_cmp = n_chunks - n_half + 1
    chunks = kv[:, :n_chunks * CMP_STRIDE].reshape(bsz, n_chunks, CMP_STRIDE, N_KV_HEADS, 2, HEAD_DIM)
    w_half = pool_w.reshape(2, n_half, CMP_STRIDE)
    part = jnp.einsum('bcjgkd,khj->hbcgkd', chunks, w_half)
    pooled = part[0, :, 0:n_cmp]
    for hf in range(1, n_half):
        pooled = pooled + part[hf, :, hf:hf + n_cmp]
    pooled = pooled + jnp.einsum('kj,kjd->kd', pool_w, pe)
    hid = jax.nn.silu(jnp.einsum('bcgkd,kde->bcgke', pooled, w1))
    return jnp.einsum('bcgke,ked->bcgkd', hid, w2)


def cmp_attend(q, ckv, q_pos):
    n_cmp = ckv.shape[1]
    s = jnp.einsum('bgrtd,bcgd->bgrtc', q, ckv[..., 0, :]) * SCALE
    last = jnp.arange(n_cmp) * CMP_STRIDE + CMP_BLOCK - 1
    p = masked_softmax(s, last[None, :] <= q_pos[:, None])
    o = jnp.einsum('bgrtc,bcgd->bgrtd', p.astype(q.dtype), ckv[..., 1, :])
    return o, p


def select_blocks(p_cmp, q_pos, n_sel):
    n_cmp = p_cmp.shape[-1]
    c_start = jnp.arange(n_cmp) * CMP_STRIDE
    b_start = jnp.arange(n_sel) * SEL_BLOCK
    overlap = (c_start[:, None] < b_start[None, :] + SEL_BLOCK) & (c_start[:, None] + CMP_BLOCK > b_start[None, :])
    imp = jnp.einsum('bgrtc,cj->bgtj', p_cmp, overlap.astype(p_cmp.dtype))
    blk = jnp.arange(n_sel)[None, :]
    cur = (q_pos // SEL_BLOCK)[:, None]
    valid = b_start[None, :] <= q_pos[:, None]
    forced = (blk == 0) | ((blk <= cur) & (blk > cur - N_LOCAL_FORCED))
    score = jnp.where(forced, jnp.inf, jnp.where(valid, imp, -jnp.inf))
    top_score, idx = lax.top_k(score, min(TOP_N, n_sel))
    return idx, top_score > -jnp.inf


def sel_attend(q, kv_g, blk_idx, blk_ok, q_pos):
    bsz, g, t, n = blk_idx.shape
    k_pos = blk_idx[..., None] * SEL_BLOCK + jnp.arange(SEL_BLOCK)
    mask = (blk_ok[..., None] & (k_pos <= q_pos[:, None, None])).reshape(bsz, g, 1, t, n * SEL_BLOCK)
    k = kv_g[..., 0, :].reshape(bsz, g, t, n * SEL_BLOCK, HEAD_DIM)
    v = kv_g[..., 1, :].reshape(bsz, g, t, n * SEL_BLOCK, HEAD_DIM)
    s = jnp.einsum('bgrtd,bgtkd->bgrtk', q, k) * SCALE
    p = masked_softmax(s, mask)
    return jnp.einsum('bgrtk,bgtkd->bgrtd', p.astype(q.dtype), v)


def win_attend(q, kv, k_pos, q_pos):
    s = jnp.einsum('bgrtd,blgd->bgrtl', q, kv[..., 0, :]) * SCALE
    rel = q_pos[:, None] - k_pos[None, :]
    mask = (rel >= 0) & (rel < WINDOW) & (k_pos[None, :] >= 0)
    p = masked_softmax(s, mask)
    return jnp.einsum('bgrtl,blgd->bgrtd', p.astype(q.dtype), kv[..., 1, :])


def nsa_merge(o, gates, w_out):
    mixed = jnp.einsum('cbgrtd,btcgr->btgrd', o, gates)
    return jnp.einsum('btgrd,grde->bte', mixed, w_out.reshape(N_KV_HEADS, GROUP, HEAD_DIM, -1))


def nsa_prompt(h, w_in, pool_w, pe, w1, w2, w_out):
    bsz, t, _ = h.shape
    q, kv_cmp, kv_sel, kv_win, gates = nsa_project(h, w_in)
    ckv = compress(kv_cmp, pool_w, pe, w1, w2)
    n_sel = t // SEL_BLOCK
    sel_blocks = kv_sel.reshape(bsz, n_sel, SEL_BLOCK, N_KV_HEADS, 2, HEAD_DIM).transpose(0, 3, 1, 2, 4, 5)
    win_pad = jnp.pad(kv_win, ((0, 0), (WINDOW, 0), (0, 0), (0, 0), (0, 0)))
    n_qb = t // Q_BLOCK
    q_blocks = q.reshape(bsz, N_KV_HEADS, GROUP, n_qb, Q_BLOCK, HEAD_DIM).transpose(3, 0, 1, 2, 4, 5)
    b_idx = jnp.arange(bsz)[:, None, None, None]
    g_idx = jnp.arange(N_KV_HEADS)[None, :, None, None]

    def one_block(args):
        c, q_c = args
        start = c * Q_BLOCK
        q_pos = start + jnp.arange(Q_BLOCK)
        o_c, p_c = cmp_attend(q_c, ckv, q_pos)
        idx, ok = select_blocks(p_c, q_pos, n_sel)
        o_s = sel_attend(q_c, sel_blocks[b_idx, g_idx, idx], idx, ok, q_pos)
        kv_w = lax.dynamic_slice_in_dim(win_pad, start, WINDOW + Q_BLOCK, axis=1)
        o_w = win_attend(q_c, kv_w, start - WINDOW + jnp.arange(WINDOW + Q_BLOCK), q_pos)
        return jnp.stack([o_c, o_s, o_w], axis=0)

    o = lax.map(one_block, (jnp.arange(n_qb), q_blocks))
    o = o.transpose(1, 2, 3, 4, 0, 5, 6).reshape(3, bsz, N_KV_HEADS, GROUP, t, HEAD_DIM)
    y = nsa_merge(o, gates, w_out)
    return y, kv_cmp, kv_sel, kv_win[:, -min(WINDOW, t):]


def nsa_sample(h, cache_cmp_kv, cache_sel_kv, win_buf, page_table, l, w_in, pool_w, pe, w1, w2, w_out):
    bsz, t, _ = h.shape
    past = page_table.shape[1] * PAGE_SIZE
    q, kv_cmp, kv_sel, kv_win, gates = nsa_project(h, w_in)
    q_pos = past + jnp.arange(t)
    past_cmp = cache_cmp_kv[l, page_table].reshape(bsz, past, N_KV_HEADS, 2, HEAD_DIM)
    ckv = compress(jnp.concatenate([past_cmp, kv_cmp], axis=1), pool_w, pe, w1, w2)
    o_c, p_c = cmp_attend(q, ckv, q_pos)
    n_sel = -(-(past + t) // SEL_BLOCK)
    idx, ok = select_blocks(p_c, q_pos, n_sel)
    b_idx = jnp.arange(bsz)[:, None, None, None]
    g_idx = jnp.arange(N_KV_HEADS)[None, :, None, None]
    n_past_blk = past // SEL_BLOCK
    blk_per_page = PAGE_SIZE // SEL_BLOCK
    past_blk = jnp.minimum(idx, n_past_blk - 1)
    page = page_table[b_idx, past_blk // blk_per_page]
    row = (past_blk % blk_per_page)[..., None] * SEL_BLOCK + jnp.arange(SEL_BLOCK)
    kv_past = cache_sel_kv[l, page[..., None], row, g_idx[..., None]]
    n_new_blk = n_sel - n_past_blk
    new_pad = jnp.pad(kv_sel, ((0, 0), (0, n_new_blk * SEL_BLOCK - t), (0, 0), (0, 0), (0, 0)))
    new_blocks = new_pad.reshape(bsz, n_new_blk, SEL_BLOCK, N_KV_HEADS, 2, HEAD_DIM).transpose(0, 3, 1, 2, 4, 5)
    kv_new = new_blocks[b_idx, g_idx, jnp.clip(idx - n_past_blk, 0, n_new_blk - 1)]
    kv_g = jnp.where((idx >= n_past_blk)[..., None, None, None], kv_new, kv_past)
    o_s = sel_attend(q, kv_g, idx, ok, q_pos)
    n_buf = win_buf.shape[1]
    kv_w = jnp.concatenate([win_buf, kv_win], axis=1)
    o_w = win_attend(q, kv_w, past - n_buf + jnp.arange(n_buf + t), q_pos)
    y = nsa_merge(jnp.stack([o_c, o_s, o_w], axis=0), gates, w_out)
    return y, kv_cmp, kv_sel, kv_w[:, -n_buf:]


def shortconv(h, buf, w_in, conv_k, w_out):
    t = h.shape[1]
    p = jnp.einsum('btd,de->bte', h, w_in)
    b_gate, c_gate, x_in = jnp.split(p, 3, axis=-1)
    u_ext = jnp.concatenate([buf, c_gate * x_in], axis=1)
    v = conv_k[0] * u_ext[:, 0:t]
    for j in range(1, CONV_W):
        v = v + conv_k[j] * u_ext[:, j:j + t]
    y = jnp.einsum('bte,ed->btd', b_gate * v, w_out)
    return y, u_ext[:, -(CONV_W - 1):]


def swiglu(h, w_in, w_out):
    g, u = jnp.split(jnp.einsum('btd,df->btf', h, w_in), 2, axis=-1)
    return jnp.einsum('btf,fd->btd', jax.nn.silu(g) * u, w_out)


def setup_inputs(seed: int = 0) -> dict:
    key = jax.random.key(seed)
    ks = jax.random.split(key, 32)
    f32 = jnp.float32
    n_pages = PAST_LEN // PAGE_SIZE
    n_pool = (DEC_BATCH * n_pages * 5) // 4

    def nrm(k, shape, scale):
        return jax.random.normal(k, shape, f32) * scale

    def gain(k, shape):
        return 1.0 + 0.02 * jax.random.normal(k, shape, f32)

    perm = jax.random.permutation(ks[6], n_pool)
    page_table = perm[:DEC_BATCH * n_pages].reshape(DEC_BATCH, n_pages).astype(jnp.int32)
    kv_row = (N_KV_HEADS, 2, HEAD_DIM)
    return {
        'x_prompt': nrm(ks[0], (BATCH, SEQ, D_MODEL), 1.0),
        'x_sample': nrm(ks[1], (DEC_BATCH, DEC_SEQ, D_MODEL), 1.0),
        'cache_cmp_kv': nrm(ks[2], (N_ATTN_LAYERS, n_pool, PAGE_SIZE) + kv_row, 1.0),
        'cache_sel_kv': nrm(ks[3], (N_ATTN_LAYERS, n_pool, PAGE_SIZE) + kv_row, 1.0),
        'state_win_kv': nrm(ks[4], (N_ATTN_LAYERS, DEC_BATCH, min(WINDOW, PAST_LEN)) + kv_row, 1.0),
        'state_conv': nrm(ks[5], (N_CONV_LAYERS, DEC_BATCH, CONV_W - 1, D_CONV), 1.0),
        'page_table': page_table,
        'attn_norm': gain(ks[7], (N_ATTN_LAYERS, D_MODEL)),
        'attn_w_in': nrm(ks[8], (N_ATTN_LAYERS, D_MODEL, ATTN_IN), D_MODEL ** -0.5),
        'attn_cmp_pool': CMP_BLOCK ** -0.5 * (1.0 + 0.1 * jax.random.normal(ks[9], (N_ATTN_LAYERS, 2, CMP_BLOCK), f32)),
        'attn_cmp_pe': nrm(ks[10], (N_ATTN_LAYERS, 2, CMP_BLOCK, HEAD_DIM), 0.5),
        'attn_cmp_w1': nrm(ks[11], (N_ATTN_LAYERS, 2, HEAD_DIM, HEAD_DIM), HEAD_DIM ** -0.5),
        'attn_cmp_w2': nrm(ks[12], (N_ATTN_LAYERS, 2, HEAD_DIM, HEAD_DIM), HEAD_DIM ** -0.5),
        'attn_w_out': nrm(ks[13], (N_ATTN_LAYERS, N_HEADS * HEAD_DIM, D_MODEL), (N_HEADS * HEAD_DIM) ** -0.5),
        'conv_norm': gain(ks[14], (N_CONV_LAYERS, D_MODEL)),
        'conv_w_in': nrm(ks[15], (N_CONV_LAYERS, D_MODEL, 3 * D_CONV), D_MODEL ** -0.5),
        'conv_kernel': nrm(ks[16], (N_CONV_LAYERS, CONV_W, D_CONV), CONV_W ** -0.5),
        'conv_w_out': nrm(ks[17], (N_CONV_LAYERS, D_CONV, D_MODEL), D_CONV ** -0.5),
        'ffn_norm': gain(ks[18], (DEPTH, D_MODEL)),
        'ffn_w_in': nrm(ks[19], (DEPTH, D_MODEL, 2 * D_FF), D_MODEL ** -0.5),
        'ffn_w_out': nrm(ks[20], (DEPTH, D_FF, D_MODEL), D_FF ** -0.5),
        'final_norm': gain(ks[21], (D_MODEL,)),
    }


def reference(x_prompt, x_sample, cache_cmp_kv, cache_sel_kv, state_win_kv, state_conv, page_table,
              attn_norm, attn_w_in, attn_cmp_pool, attn_cmp_pe, attn_cmp_w1, attn_cmp_w2, attn_w_out,
              conv_norm, conv_w_in, conv_kernel, conv_w_out, ffn_norm, ffn_w_in, ffn_w_out, final_norm):
    xp, xs = x_prompt, x_sample
    cmp_p, sel_p, win_p, conv_p = [], [], [], []
    cmp_s, sel_s, win_s, conv_s = [], [], [], []
    for i in range(DEPTH):
        l = i // 2
        if i % 2 == 0:
            y_p, c_new, s_new, w_new = nsa_prompt(
                rmsnorm(xp, attn_norm[l]), attn_w_in[l], attn_cmp_pool[l], attn_cmp_pe[l],
                attn_cmp_w1[l], attn_cmp_w2[l], attn_w_out[l])
            cmp_p.append(c_new)
            sel_p.append(s_new)
            win_p.append(w_new)
            y_s, c_new, s_new, w_new = nsa_sample(
                rmsnorm(xs, attn_norm[l]), cache_cmp_kv, cache_sel_kv, state_win_kv[l], page_table, l,
                attn_w_in[l], attn_cmp_pool[l], attn_cmp_pe[l], attn_cmp_w1[l], attn_cmp_w2[l], attn_w_out[l])
            cmp_s.append(c_new)
            sel_s.append(s_new)
            win_s.append(w_new)
        else:
            zero_buf = jnp.zeros((xp.shape[0], CONV_W - 1, D_CONV), xp.dtype)
            y_p, b_new = shortconv(rmsnorm(xp, conv_norm[l]), zero_buf, conv_w_in[l], conv_kernel[l], conv_w_out[l])
            conv_p.append(b_new)
            y_s, b_new = shortconv(rmsnorm(xs, conv_norm[l]), state_conv[l], conv_w_in[l], conv_kernel[l], conv_w_out[l])
            conv_s.append(b_new)
        xp = xp + y_p
        xs = xs + y_s
        xp = xp + swiglu(rmsnorm(xp, ffn_norm[i]), ffn_w_in[i], ffn_w_out[i])
        xs = xs + swiglu(rmsnorm(xs, ffn_norm[i]), ffn_w_in[i], ffn_w_out[i])
    return (rmsnorm(xp, final_norm), rmsnorm(xs, final_norm),
            jnp.stack(cmp_p), jnp.stack(sel_p), jnp.stack(win_p), jnp.stack(conv_p),
            jnp.stack(cmp_s), jnp.stack(sel_s), jnp.stack(win_s), jnp.stack(conv_s))
```

```python
import functools

import jax
import jax.numpy as jnp
from jax import lax
from jax.experimental import pallas as pl
from jax.experimental.pallas import tpu as pltpu

N_HEADS = 16
HEAD_DIM = 128
N_KV_HEADS = 4
GROUP = N_HEADS // N_KV_HEADS
KV_COLS = N_KV_HEADS * 2 * HEAD_DIM
CMP_BLOCK = 32
CMP_STRIDE = 16
SEL_BLOCK = 64
SEL_SHIFT = 6
TOP_N = 16
N_LOCAL_FORCED = 2
WINDOW = 512
Q_BLOCK = 64
PAGE_SIZE = 128
CONV_W = 3
EPS = 1e-6
SCALE = HEAD_DIM ** -0.5
NEG = -1e30

LANES = 128
SEL_CHUNK = 512
WIN_KEYS = WINDOW + 2 * Q_BLOCK
VMEM_LIMIT = 56 * 1024 * 1024

F32 = jnp.float32
BF16 = jnp.bfloat16


def _params(n_axes):
    return pltpu.CompilerParams(dimension_semantics=("arbitrary",) * n_axes,
                                vmem_limit_bytes=VMEM_LIMIT)


def _pick_tile(n, limit, mult):
    best = None
    for t in range(mult, min(n, limit) + 1, mult):
        if n % t == 0:
            best = t
    assert best is not None, (n, limit, mult)
    return best


def _dot(a, b):
    return jnp.dot(a, b, preferred_element_type=F32)


def _dot_nt(a, b):
    return lax.dot_general(a, b, (((1,), (1,)), ((), ())), preferred_element_type=F32)


def _silu(x):
    return x * jax.nn.sigmoid(x)


def _rmsnorm_kernel(x_ref, g_ref, o_ref):
    x = x_ref[...]
    ms = jnp.mean(x * x, axis=-1, keepdims=True)
    o_ref[...] = ((x * lax.rsqrt(ms + EPS)) * g_ref[...]).astype(o_ref.dtype)


def _rmsnorm(x, gain, out_dtype):
    m, d = x.shape
    tm = _pick_tile(m, 1024, 16)
    return pl.pallas_call(
        _rmsnorm_kernel,
        grid=(m // tm,),
        in_specs=[pl.BlockSpec((tm, d), lambda i: (i, 0)),
                  pl.BlockSpec((1, d), lambda i: (0, 0))],
        out_specs=pl.BlockSpec((tm, d), lambda i: (i, 0)),
        out_shape=jax.ShapeDtypeStruct((m, d), out_dtype),
        compiler_params=_params(1),
    )(x, gain.reshape(1, d))


def _mm_kernel(a_ref, b_ref, o_ref):
    o_ref[...] = _dot(a_ref[...], b_ref[...])


def _matmul(a, b, n_cols=None, col_block0=0):
    m, k = a.shape
    n = b.shape[1] if n_cols is None else n_cols
    tm = _pick_tile(m, 1024, 16)
    tn = _pick_tile(n, 512, LANES)
    return pl.pallas_call(
        _mm_kernel,
        grid=(n // tn, m // tm),
        in_specs=[pl.BlockSpec((tm, k), lambda j, i: (i, 0)),
                  pl.BlockSpec((k, tn), lambda j, i: (0, j + col_block0))],
        out_specs=pl.BlockSpec((tm, tn), lambda j, i: (i, j)),
        out_shape=jax.ShapeDtypeStruct((m, n), F32),
        compiler_params=_params(2),
    )(a, b)


def _mm_res_kernel(a_ref, b_ref, r_ref, o_ref):
    kk = pl.program_id(2)
    part = _dot(a_ref[...], b_ref[...])

    @pl.when(kk == 0)
    def _():
        o_ref[...] = r_ref[...] + part

    @pl.when(kk > 0)
    def _():
        o_ref[...] += part


def _matmul_residual(a, b, res):
    m, k = a.shape
    n = b.shape[1]
    tm = _pick_tile(m, 1024, 16)
    tn = _pick_tile(n, 512, LANES)
    tk = _pick_tile(k, 3072, LANES)
    return pl.pallas_call(
        _mm_res_kernel,
        grid=(n // tn, m // tm, k // tk),
        in_specs=[pl.BlockSpec((tm, tk), lambda j, i, kk: (i, kk)),
                  pl.BlockSpec((tk, tn), lambda j, i, kk: (kk, j)),
                  pl.BlockSpec((tm, tn), lambda j, i, kk: (i, j))],
        out_specs=pl.BlockSpec((tm, tn), lambda j, i, kk: (i, j)),
        out_shape=jax.ShapeDtypeStruct((m, n), F32),
        compiler_params=_params(3),
    )(a, b, res)


def _mm_swiglu_kernel(a_ref, bg_ref, bu_ref, o_ref):
    a = a_ref[...]
    gate = _dot(a, bg_ref[...])
    up = _dot(a, bu_ref[...])
    o_ref[...] = (_silu(gate) * up).astype(o_ref.dtype)


def _matmul_swiglu(a, w):
    m, k = a.shape
    f = w.shape[1] // 2
    tm = _pick_tile(m, 1024, 16)
    tn = _pick_tile(f, 512, LANES)
    nb = f // tn
    return pl.pallas_call(
        _mm_swiglu_kernel,
        grid=(nb, m // tm),
        in_specs=[pl.BlockSpec((tm, k), lambda j, i: (i, 0)),
                  pl.BlockSpec((k, tn), lambda j, i: (0, j)),
                  pl.BlockSpec((k, tn), lambda j, i: (0, j + nb))],
        out_specs=pl.BlockSpec((tm, tn), lambda j, i: (i, j)),
        out_shape=jax.ShapeDtypeStruct((m, f), BF16),
        compiler_params=_params(2),
    )(a, w, w)


def _mm_conv_in_kernel(a_ref, wb_ref, wc_ref, wx_ref, b_ref, u_ref):
    a = a_ref[...]
    b_ref[...] = _dot(a, wb_ref[...])
    u_ref[...] = _dot(a, wc_ref[...]) * _dot(a, wx_ref[...])


def _matmul_conv_in(a, w):
    m, k = a.shape
    dc = w.shape[1] // 3
    tm = _pick_tile(m, 1024, 16)
    tn = _pick_tile(dc, 512, LANES)
    nb = dc // tn
    out = jax.ShapeDtypeStruct((m, dc), F32)
    return pl.pallas_call(
        _mm_conv_in_kernel,
        grid=(nb, m // tm),
        in_specs=[pl.BlockSpec((tm, k), lambda j, i: (i, 0)),
                  pl.BlockSpec((k, tn), lambda j, i: (0, j)),
                  pl.BlockSpec((k, tn), lambda j, i: (0, j + nb)),
                  pl.BlockSpec((k, tn), lambda j, i: (0, j + 2 * nb))],
        out_specs=[pl.BlockSpec((tm, tn), lambda j, i: (i, j)),
                   pl.BlockSpec((tm, tn), lambda j, i: (i, j))],
        out_shape=[out, out],
        compiler_params=_params(2),
    )(a, w, w, w)


def _conv_apply_kernel(u_ref, b_ref, prev_ref, hist_ref, k_ref, o_ref):
    i = pl.program_id(1)
    u = u_ref[...]
    rows = u.shape[0]
    tail = jnp.where(i == 0, hist_ref[...], prev_ref[...])
    row = lax.broadcasted_iota(jnp.int32, u.shape, 0)
    u1 = jnp.where(row == 0, tail[7:8], pltpu.roll(u, 1, axis=0))
    u2 = jnp.where(row == 0, tail[6:7], jnp.where(row == 1, tail[7:8], pltpu.roll(u, 2 % rows, axis=0)))
    k = k_ref[...]
    v = k[0:1] * u2 + k[1:2] * u1 + k[2:3] * u
    o_ref[...] = (b_ref[...] * v).astype(o_ref.dtype)


def _conv_apply(u, bgate, hist, conv_k, row0, n_seq, seq_len, out_dtype):
    d = u.shape[1]
    tt = _pick_tile(seq_len, 512, 8)
    tc = _pick_tile(d, 1024, LANES)
    tiles = seq_len // tt
    blk0 = row0 // tt
    sub = tt // 8

    def cur(s, i, j):
        return (blk0 + s * tiles + i, j)

    def prev(s, i, j):
        return (jnp.maximum((blk0 + s * tiles + i) * sub - 1, 0), j)

    return pl.pallas_call(
        _conv_apply_kernel,
        grid=(n_seq, tiles, d // tc),
        in_specs=[pl.BlockSpec((tt, tc), cur),
                  pl.BlockSpec((tt, tc), cur),
                  pl.BlockSpec((8, tc), prev),
                  pl.BlockSpec((None, 8, tc), lambda s, i, j: (s, 0, j)),
                  pl.BlockSpec((8, tc), lambda s, i, j: (0, j))],
        out_specs=pl.BlockSpec((tt, tc), lambda s, i, j: (s * tiles + i, j)),
        out_shape=jax.ShapeDtypeStruct((n_seq * seq_len, d), out_dtype),
        compiler_params=_params(3),
    )(u, bgate, u, hist, conv_k)


def _halfsum_kernel(x_ref, wa_ref, wb_ref, a_ref, b_ref):
    x = x_ref[...]
    n = x.shape[0] // CMP_STRIDE
    x3 = x.reshape(n, CMP_STRIDE, x.shape[1])
    a_ref[...] = jnp.sum(x3 * wa_ref[...][None], axis=1)
    b_ref[...] = jnp.sum(x3 * wb_ref[...][None], axis=1)


def _halfsum_paged_kernel(pt_ref, x_ref, wa_ref, wb_ref, a_ref, b_ref):
    del pt_ref
    _halfsum_kernel(x_ref, wa_ref, wb_ref, a_ref, b_ref)


def _pool_weight_maps(pool_w):
    w = pool_w.reshape(2, CMP_BLOCK // CMP_STRIDE, CMP_STRIDE)
    maps = []
    for half in range(2):
        wk = jnp.broadcast_to(w[:, half, :].T[:, None, :, None], (CMP_STRIDE, N_KV_HEADS, 2, HEAD_DIM))
        maps.append(wk.reshape(CMP_STRIDE, KV_COLS))
    return maps


def _halfsums_prompt(proj, pool_w, n_rows, col_block):
    wa, wb = _pool_weight_maps(pool_w)
    tr = _pick_tile(n_rows, 512, LANES)
    out = jax.ShapeDtypeStruct((n_rows // CMP_STRIDE, KV_COLS), F32)
    wspec = pl.BlockSpec((CMP_STRIDE, KV_COLS), lambda i: (0, 0))
    ospec = pl.BlockSpec((tr // CMP_STRIDE, KV_COLS), lambda i: (i, 0))
    return pl.pallas_call(
        _halfsum_kernel,
        grid=(n_rows // tr,),
        in_specs=[pl.BlockSpec((tr, KV_COLS), lambda i: (i, col_block)), wspec, wspec],
        out_specs=[ospec, ospec],
        out_shape=[out, out],
        compiler_params=_params(1),
    )(proj, wa, wb)


def _halfsums_paged(cache, layer, page_table, pool_w):
    bs, n_pages = page_table.shape
    wa, wb = _pool_weight_maps(pool_w)
    cpp = PAGE_SIZE // CMP_STRIDE
    out = jax.ShapeDtypeStruct((bs, n_pages * cpp, KV_COLS), F32)
    wspec = pl.BlockSpec((CMP_STRIDE, KV_COLS), lambda b, p, pt: (0, 0))
    ospec = pl.BlockSpec((None, cpp, KV_COLS), lambda b, p, pt: (b, p, 0))
    grid_spec = pltpu.PrefetchScalarGridSpec(
        num_scalar_prefetch=1,
        grid=(bs, n_pages),
        in_specs=[pl.BlockSpec((None, None, PAGE_SIZE, KV_COLS), lambda b, p, pt: (layer, pt[b, p], 0, 0)),
                  wspec, wspec],
        out_specs=[ospec, ospec],
    )
    return pl.pallas_call(
        _halfsum_paged_kernel,
        grid_spec=grid_spec,
        out_shape=[out, out],
        compiler_params=_params(2),
    )(page_table, cache, wa, wb)


def _cmp_mlp_kernel(a_ref, b_ref, pw_ref, pe_ref, w1_ref, w2_ref, o_ref):
    n = a_ref.shape[0]
    pooled = a_ref[...] + pltpu.roll(b_ref[...], n - 1, axis=0)
    for k in range(2):
        pe_term = jnp.sum(pw_ref[k] * pe_ref[k], axis=0, keepdims=True)
        w1 = w1_ref[k].astype(BF16)
        w2 = w2_ref[k].astype(BF16)
        for g in range(N_KV_HEADS):
            c0 = (g * 2 + k) * HEAD_DIM
            x = pooled[:, c0:c0 + HEAD_DIM] + pe_term
            hid = _silu(_dot(x.astype(BF16), w1))
            o_ref[:, c0:c0 + HEAD_DIM] = _dot(hid.astype(BF16), w2)


def _cmp_mlp(half_a, half_b, pool_w, pe, w1, w2):
    n_seq, n_chunks, _ = half_a.shape
    spec = pl.BlockSpec((None, n_chunks, KV_COLS), lambda b: (b, 0, 0))
    full3 = lambda shape: pl.BlockSpec(shape, lambda b: (0, 0, 0))
    return pl.pallas_call(
        _cmp_mlp_kernel,
        grid=(n_seq,),
        in_specs=[spec, spec, full3((2, CMP_BLOCK, 1)), full3((2, CMP_BLOCK, HEAD_DIM)),
                  full3((2, HEAD_DIM, HEAD_DIM)), full3((2, HEAD_DIM, HEAD_DIM))],
        out_specs=spec,
        out_shape=jax.ShapeDtypeStruct((n_seq, n_chunks, KV_COLS), F32),
        compiler_params=_params(1),
    )(half_a, half_b, pool_w.reshape(2, CMP_BLOCK, 1), pe, w1, w2)


def _overlap_matrix(n_cmp_pad, n_sel_pad):
    c_start = jnp.arange(n_cmp_pad)[:, None] * CMP_STRIDE
    b_start = jnp.arange(n_sel_pad)[None, :] * SEL_BLOCK
    ov = (c_start < b_start + SEL_BLOCK) & (c_start + CMP_BLOCK > b_start)
    return ov.astype(BF16)


def _stack_heads(q):
    return jnp.concatenate([q[:, r * HEAD_DIM:(r + 1) * HEAD_DIM] for r in range(GROUP)], axis=0)


def _topk_mask_by_rank(score, k):
    t, n = score.shape
    col = lax.broadcasted_iota(jnp.int32, (t, n), 1)
    rank = jnp.zeros((t, n), F32)
    for i in range(n):
        si = score[:, i:i + 1]
        ahead = (si > score) | ((si == score) & (col > i))
        rank = rank + jnp.where(ahead, 1.0, 0.0)
    return jnp.where((rank < k) & (score > -jnp.inf), 1.0, 0.0)


def _nsa_prompt_kernel(q_ref, sel_ref, win_ref, ckv_ref, gate_ref, ov_ref, exp_ref, o_ref,
                       ks_scr, vs_scr, kw_scr, vw_scr):
    c = pl.program_id(2)
    tq = Q_BLOCK
    n_cmp = ckv_ref.shape[0]
    n_sel = ov_ref.shape[1]

    @pl.when(c == 0)
    def _():
        ks_scr[...] = sel_ref[:, 0:HEAD_DIM].astype(BF16)
        vs_scr[...] = sel_ref[:, HEAD_DIM:2 * HEAD_DIM].astype(BF16)
        kw_scr[...] = win_ref[:, 0:HEAD_DIM].astype(BF16)
        vw_scr[...] = win_ref[:, HEAD_DIM:2 * HEAD_DIM].astype(BF16)

    start = c * tq
    q4 = _stack_heads(q_ref[...]).astype(BF16)
    tpos = start + lax.broadcasted_iota(jnp.int32, (tq, 1), 0)

    kc = ckv_ref[:, 0:HEAD_DIM].astype(BF16)
    vc = ckv_ref[:, HEAD_DIM:2 * HEAD_DIM].astype(BF16)
    s = (_dot_nt(q4, kc) * SCALE).reshape(GROUP, tq, n_cmp)
    last = lax.broadcasted_iota(jnp.int32, (1, n_cmp), 1) * CMP_STRIDE + (CMP_BLOCK - 1)
    ok = (last <= tpos)[None]
    sm = jnp.where(ok, s, NEG)
    m = jnp.max(sm, axis=-1, keepdims=True)
    e = jnp.where(ok, jnp.exp(sm - m), 0.0)
    p = e / jnp.maximum(jnp.sum(e, axis=-1, keepdims=True), 1e-30)
    pb = p.astype(BF16).reshape(GROUP * tq, n_cmp)
    o_c = _dot(pb, vc).reshape(GROUP, tq, HEAD_DIM)

    imp = jnp.sum(_dot(pb, ov_ref[...]).reshape(GROUP, tq, n_sel), axis=0)
    blk = lax.broadcasted_iota(jnp.int32, (tq, n_sel), 1)
    cur = tpos >> SEL_SHIFT
    valid = blk * SEL_BLOCK <= tpos
    forced = (blk == 0) | ((blk <= cur) & (blk > cur - N_LOCAL_FORCED))
    score = jnp.where(forced, jnp.inf, jnp.where(valid, imp, -jnp.inf))
    selb = _topk_mask_by_rank(score, min(TOP_N, n_sel)).astype(BF16)

    def sel_step(kc_i, carry):
        m_i, l_i, acc = carry
        off = pl.multiple_of(kc_i * SEL_CHUNK, SEL_CHUNK)
        k = ks_scr[pl.ds(off, SEL_CHUNK), :]
        v = vs_scr[pl.ds(off, SEL_CHUNK), :]
        s_i = (_dot_nt(q4, k) * SCALE).reshape(GROUP, tq, SEL_CHUNK)
        picked = _dot(selb, exp_ref[kc_i])
        kpos = off + lax.broadcasted_iota(jnp.int32, (1, SEL_CHUNK), 1)
        bias = jnp.where((picked > 0.5) & (kpos <= tpos), 0.0, NEG)
        s_i = s_i + bias[None]
        m_new = jnp.maximum(m_i, jnp.max(s_i, axis=-1, keepdims=True))
        alpha = jnp.exp(m_i - m_new)
        p_i = jnp.exp(s_i - m_new)
        l_new = alpha * l_i + jnp.sum(p_i, axis=-1, keepdims=True)
        pv = _dot(p_i.astype(BF16).reshape(GROUP * tq, SEL_CHUNK), v).reshape(GROUP, tq, HEAD_DIM)
        return m_new, l_new, alpha * acc + pv

    n_steps = (start + tq + SEL_CHUNK - 1) // SEL_CHUNK
    init = (jnp.full((GROUP, tq, 1), NEG, F32), jnp.zeros((GROUP, tq, 1), F32),
            jnp.zeros((GROUP, tq, HEAD_DIM), F32))
    _, l_s, acc_s = lax.fori_loop(0, n_steps, sel_step, init)
    o_s = acc_s / l_s

    lo = pl.multiple_of(jnp.maximum(start + tq - WIN_KEYS, 0), Q_BLOCK)
    kw = kw_scr[pl.ds(lo, WIN_KEYS), :]
    vw = vw_scr[pl.ds(lo, WIN_KEYS), :]
    s_w = (_dot_nt(q4, kw) * SCALE).reshape(GROUP, tq, WIN_KEYS)
    rel = tpos - (lo + lax.broadcasted_iota(jnp.int32, (1, WIN_KEYS), 1))
    s_w = s_w + jnp.where((rel >= 0) & (rel < WINDOW), 0.0, NEG)[None]
    p_w = jnp.exp(s_w - jnp.max(s_w, axis=-1, keepdims=True))
    l_w = jnp.sum(p_w, axis=-1, keepdims=True)
    o_w = _dot(p_w.astype(BF16).reshape(GROUP * tq, WIN_KEYS), vw).reshape(GROUP, tq, HEAD_DIM) / l_w

    gate = jax.nn.sigmoid(gate_ref[...])
    outs = []
    for r in range(GROUP):
        outs.append(gate[:, r:r + 1] * o_c[r]
                    + gate[:, GROUP + r:GROUP + r + 1] * o_s[r]
                    + gate[:, 2 * GROUP + r:2 * GROUP + r + 1] * o_w[r])
    o_ref[...] = jnp.concatenate(outs, axis=1).astype(o_ref.dtype)


def _nsa_prompt(proj, gates, ckv, n_batch, seq):
    n_qb = seq // Q_BLOCK
    n_cmp = seq // CMP_STRIDE
    n_sel = seq // SEL_BLOCK
    n_chunks = seq // SEL_CHUNK
    qw = GROUP * HEAD_DIM
    kvw = 2 * HEAD_DIM
    q_cols = N_HEADS * HEAD_DIM
    sel_blk0 = (q_cols + KV_COLS) // kvw
    win_blk0 = (q_cols + 2 * KV_COLS) // kvw
    overlap = _overlap_matrix(n_cmp, n_sel)
    key_blk = jnp.arange(seq) // SEL_BLOCK
    expand = (jnp.arange(n_sel)[:, None] == key_blk[None, :]).astype(BF16)
    expand = expand.reshape(n_sel, n_chunks, SEL_CHUNK).transpose(1, 0, 2)
    kv_scr = pltpu.VMEM((seq, HEAD_DIM), BF16)
    return pl.pallas_call(
        _nsa_prompt_kernel,
        grid=(n_batch, N_KV_HEADS, n_qb),
        in_specs=[pl.BlockSpec((Q_BLOCK, qw), lambda b, g, c: (b * n_qb + c, g)),
                  pl.BlockSpec((seq, kvw), lambda b, g, c: (b, sel_blk0 + g)),
                  pl.BlockSpec((seq, kvw), lambda b, g, c: (b, win_blk0 + g)),
                  pl.BlockSpec((None, n_cmp, kvw), lambda b, g, c: (b, 0, g)),
                  pl.BlockSpec((Q_BLOCK, LANES), lambda b, g, c: (b * n_qb + c, g)),
                  pl.BlockSpec((n_cmp, n_sel), lambda b, g, c: (0, 0)),
                  pl.BlockSpec((n_chunks, n_sel, SEL_CHUNK), lambda b, g, c: (0, 0, 0))],
        out_specs=pl.BlockSpec((Q_BLOCK, qw), lambda b, g, c: (b * n_qb + c, g)),
        out_shape=jax.ShapeDtypeStruct((n_batch * seq, N_HEADS * HEAD_DIM), BF16),
        scratch_shapes=[kv_scr, kv_scr, kv_scr, kv_scr],
        compiler_params=_params(3),
    )(proj, proj, proj, ckv, gates, overlap, expand)


def _topk_mask_by_extraction(score, k):
    t, n = score.shape
    col = lax.broadcasted_iota(jnp.int32, (t, n), 1).astype(F32)
    sel = jnp.zeros((t, n), F32)
    for _ in range(k):
        m = jnp.max(score, axis=-1, keepdims=True)
        first = jnp.min(jnp.where(score == m, col, float(n)), axis=-1, keepdims=True)
        pick = (col == first) & (m > -jnp.inf)
        sel = jnp.where(pick, 1.0, sel)
        score = jnp.where(col == first, -jnp.inf, score)
    return sel


def _nsa_sample_cmp_kernel(q_ref, ckv_ref, ov_ref, oc_ref, sel_ref, *, past, n_sel):
    ts = q_ref.shape[0]
    n_cmp = ckv_ref.shape[0]
    n_sel_pad = ov_ref.shape[1]
    q4 = _stack_heads(q_ref[...]).astype(BF16)
    tpos = past + lax.broadcasted_iota(jnp.int32, (ts, 1), 0)
    kc = ckv_ref[:, 0:HEAD_DIM].astype(BF16)
    vc = ckv_ref[:, HEAD_DIM:2 * HEAD_DIM].astype(BF16)
    s = (_dot_nt(q4, kc) * SCALE).reshape(GROUP, ts, n_cmp)
    last = lax.broadcasted_iota(jnp.int32, (1, n_cmp), 1) * CMP_STRIDE + (CMP_BLOCK - 1)
    ok = (last <= tpos)[None]
    sm = jnp.where(ok, s, NEG)
    m = jnp.max(sm, axis=-1, keepdims=True)
    e = jnp.where(ok, jnp.exp(sm - m), 0.0)
    p = e / jnp.maximum(jnp.sum(e, axis=-1, keepdims=True), 1e-30)
    pb = p.astype(BF16).reshape(GROUP * ts, n_cmp)
    o_c = _dot(pb, vc).reshape(GROUP, ts, HEAD_DIM)
    oc_ref[...] = jnp.concatenate([o_c[r] for r in range(GROUP)], axis=1)

    imp = jnp.sum(_dot(pb, ov_ref[...]).reshape(GROUP, ts, n_sel_pad), axis=0)
    blk = lax.broadcasted_iota(jnp.int32, (ts, n_sel_pad), 1)
    cur = tpos >> SEL_SHIFT
    valid = (blk * SEL_BLOCK <= tpos) & (blk < n_sel)
    forced = ((blk == 0) | ((blk <= cur) & (blk > cur - N_LOCAL_FORCED))) & (blk < n_sel)
    score = jnp.where(forced, jnp.inf, jnp.where(valid, imp, -jnp.inf))
    sel_ref[...] = _topk_mask_by_extraction(score, min(TOP_N, n_sel))


def _nsa_sample_cmp(proj, ckv, row0, bs, ts, past):
    n_cmp = ckv.shape[1]
    n_sel = -(-(past + ts) // SEL_BLOCK)
    n_sel_pad = -(-n_sel // LANES) * LANES
    qw = GROUP * HEAD_DIM
    kvw = 2 * HEAD_DIM
    overlap = _overlap_matrix(n_cmp, n_sel_pad)
    kern = functools.partial(_nsa_sample_cmp_kernel, past=past, n_sel=n_sel)
    return pl.pallas_call(
        kern,
        grid=(bs, N_KV_HEADS),
        in_specs=[pl.BlockSpec((ts, qw), lambda b, g: (row0 // ts + b, g)),
                  pl.BlockSpec((None, n_cmp, kvw), lambda b, g: (b, 0, g)),
                  pl.BlockSpec((n_cmp, n_sel_pad), lambda b, g: (0, 0))],
        out_specs=[pl.BlockSpec((ts, qw), lambda b, g: (b, g)),
                   pl.BlockSpec((None, None, ts, n_sel_pad), lambda b, g: (b, g, 0, 0))],
        out_shape=[jax.ShapeDtypeStruct((bs * ts, N_HEADS * HEAD_DIM), F32),
                   jax.ShapeDtypeStruct((bs, N_KV_HEADS, ts, n_sel_pad), F32)],
        compiler_params=_params(2),
    )(proj, ckv, overlap)


def _nsa_sample_attn_kernel(pt_ref, q_ref, page_ref, newsel_ref, selmask_ref, winbuf_ref, newwin_ref,
                            oc_ref, gate_ref, o_ref, m_scr, l_scr, acc_scr, *, past, n_buf):
    del pt_ref
    p_id = pl.program_id(1)
    n_pages = pl.num_programs(1) - 1
    ts = q_ref.shape[0]
    n_sel_pad = selmask_ref.shape[2]
    tpos = past + lax.broadcasted_iota(jnp.int32, (ts, 1), 0)
    q = q_ref[...]

    def q_heads(g):
        return _stack_heads(q[:, g * GROUP * HEAD_DIM:(g + 1) * GROUP * HEAD_DIM]).astype(BF16)

    @pl.when(p_id == 0)
    def _():
        m_scr[...] = jnp.full(m_scr.shape, NEG, F32)
        l_scr[...] = jnp.zeros(l_scr.shape, F32)
        acc_scr[...] = jnp.zeros(acc_scr.shape, F32)

    def sel_update(kv_ref):
        blk = lax.broadcasted_iota(jnp.int32, (n_sel_pad, PAGE_SIZE), 0)
        key = lax.broadcasted_iota(jnp.int32, (n_sel_pad, PAGE_SIZE), 1)
        expand = jnp.where(blk == p_id * (PAGE_SIZE // SEL_BLOCK) + (key >> SEL_SHIFT), 1.0, 0.0).astype(BF16)
        kpos = p_id * PAGE_SIZE + lax.broadcasted_iota(jnp.int32, (1, PAGE_SIZE), 1)
        for g in range(N_KV_HEADS):
            c0 = g * 2 * HEAD_DIM
            k = kv_ref[:, c0:c0 + HEAD_DIM].astype(BF16)
            v = kv_ref[:, c0 + HEAD_DIM:c0 + 2 * HEAD_DIM].astype(BF16)
            s = (_dot_nt(q_heads(g), k) * SCALE).reshape(GROUP, ts, PAGE_SIZE)
            picked = _dot(selmask_ref[g].astype(BF16), expand)
            bias = jnp.where((picked > 0.5) & (kpos <= tpos), 0.0, NEG)
            s = s + bias[None]
            m_old = m_scr[g]
            m_new = jnp.maximum(m_old, jnp.max(s, axis=-1, keepdims=True))
            alpha = jnp.exp(m_old - m_new)
            p = jnp.exp(s - m_new)
            l_scr[g] = alpha * l_scr[g] + jnp.sum(p, axis=-1, keepdims=True)
            pv = _dot(p.astype(BF16).reshape(GROUP * ts, PAGE_SIZE), v).reshape(GROUP, ts, HEAD_DIM)
            acc_scr[g] = alpha * acc_scr[g] + pv
            m_scr[g] = m_new

    @pl.when(p_id < n_pages)
    def _():
        sel_update(page_ref)

    @pl.when(p_id == n_pages)
    def _():
        sel_update(newsel_ref)
        gate = jax.nn.sigmoid(gate_ref[...])
        i_buf = lax.broadcasted_iota(jnp.int32, (1, n_buf), 1)
        rel_buf = tpos - (past - n_buf + i_buf)
        bias_buf = jnp.where((rel_buf >= 0) & (rel_buf < WINDOW), 0.0, NEG)[None]
        i_new = lax.broadcasted_iota(jnp.int32, (1, PAGE_SIZE), 1)
        rel_new = tpos - (past + i_new)
        bias_new = jnp.where((rel_new >= 0) & (rel_new < WINDOW), 0.0, NEG)[None]
        outs = []
        for g in range(N_KV_HEADS):
            c0 = g * 2 * HEAD_DIM
            qg = q_heads(g)
            o_s = acc_scr[g] / l_scr[g]
            k1 = winbuf_ref[:, c0:c0 + HEAD_DIM].astype(BF16)
            v1 = winbuf_ref[:, c0 + HEAD_DIM:c0 + 2 * HEAD_DIM].astype(BF16)
            k2 = newwin_ref[:, c0:c0 + HEAD_DIM].astype(BF16)
            v2 = newwin_ref[:, c0 + HEAD_DIM:c0 + 2 * HEAD_DIM].astype(BF16)
            s1 = (_dot_nt(qg, k1) * SCALE).reshape(GROUP, ts, n_buf) + bias_buf
            s2 = (_dot_nt(qg, k2) * SCALE).reshape(GROUP, ts, PAGE_SIZE) + bias_new
            m = jnp.maximum(jnp.max(s1, axis=-1, keepdims=True), jnp.max(s2, axis=-1, keepdims=True))
            p1 = jnp.exp(s1 - m)
            p2 = jnp.exp(s2 - m)
            l = jnp.sum(p1, axis=-1, keepdims=True) + jnp.sum(p2, axis=-1, keepdims=True)
            o_w = (_dot(p1.astype(BF16).reshape(GROUP * ts, n_buf), v1)
                   + _dot(p2.astype(BF16).reshape(GROUP * ts, PAGE_SIZE), v2)).reshape(GROUP, ts, HEAD_DIM) / l
            for r in range(GROUP):
                h0 = (g * GROUP + r) * HEAD_DIM
                gl = g * LANES
                outs.append(gate[:, gl + r:gl + r + 1] * oc_ref[:, h0:h0 + HEAD_DIM]
                            + gate[:, gl + GROUP + r:gl + GROUP + r + 1] * o_s[r]
                            + gate[:, gl + 2 * GROUP + r:gl + 2 * GROUP + r + 1] * o_w[r])
        o_ref[...] = jnp.concatenate(outs, axis=1)


def _nsa_sample_attn(proj, gates, cache_sel, layer, page_table, selmask, win_buf, o_c, row0, bs, ts, past):
    n_pages = page_table.shape[1]
    n_buf = win_buf.shape[1]
    n_sel_pad = selmask.shape[3]
    q_cols = N_HEADS * HEAD_DIM
    rb = row0 // ts
    pad = ((0, 0), (0, PAGE_SIZE - ts), (0, 0))
    new_sel = jnp.pad(proj[row0:, q_cols + KV_COLS:q_cols + 2 * KV_COLS].reshape(bs, ts, KV_COLS), pad)
    new_win = jnp.pad(proj[row0:, q_cols + 2 * KV_COLS:q_cols + 3 * KV_COLS].reshape(bs, ts, KV_COLS), pad)
    kern = functools.partial(_nsa_sample_attn_kernel, past=past, n_buf=n_buf)
    grid_spec = pltpu.PrefetchScalarGridSpec(
        num_scalar_prefetch=1,
        grid=(bs, n_pages + 1),
        in_specs=[pl.BlockSpec((ts, q_cols), lambda b, p, pt: (rb + b, 0)),
                  pl.BlockSpec((None, None, PAGE_SIZE, KV_COLS),
                               lambda b, p, pt: (layer, pt[b, jnp.minimum(p, n_pages - 1)], 0, 0)),
                  pl.BlockSpec((None, PAGE_SIZE, KV_COLS), lambda b, p, pt: (b, 0, 0)),
                  pl.BlockSpec((None, N_KV_HEADS, ts, n_sel_pad), lambda b, p, pt: (b, 0, 0, 0)),
                  pl.BlockSpec((None, n_buf, KV_COLS), lambda b, p, pt: (b, 0, 0)),
                  pl.BlockSpec((None, PAGE_SIZE, KV_COLS), lambda b, p, pt: (b, 0, 0)),
                  pl.BlockSpec((ts, q_cols), lambda b, p, pt: (b, 0)),
                  pl.BlockSpec((ts, N_KV_HEADS * LANES), lambda b, p, pt: (rb + b, 0))],
        out_specs=pl.BlockSpec((ts, q_cols), lambda b, p, pt: (b, 0)),
        scratch_shapes=[pltpu.VMEM((N_KV_HEADS, GROUP, ts, 1), F32),
                        pltpu.VMEM((N_KV_HEADS, GROUP, ts, 1), F32),
                        pltpu.VMEM((N_KV_HEADS, GROUP, ts, HEAD_DIM), F32)],
    )
    return pl.pallas_call(
        kern,
        grid_spec=grid_spec,
        out_shape=jax.ShapeDtypeStruct((bs * ts, q_cols), F32),
        compiler_params=_params(2),
    )(page_table, proj, cache_sel, new_sel, selmask, win_buf, new_win, o_c, gates)


def _gate_weight(w_in):
    q_cols = N_HEADS * HEAD_DIM
    wg = w_in[:, q_cols + 3 * KV_COLS:].reshape(-1, 3, N_KV_HEADS, GROUP).transpose(0, 2, 1, 3)
    wg = wg.reshape(-1, N_KV_HEADS, 3 * GROUP)
    wg = jnp.pad(wg, ((0, 0), (0, 0), (0, LANES - 3 * GROUP)))
    return wg.reshape(-1, N_KV_HEADS * LANES).astype(BF16)


def _attn_layer(x, layer, shapes, cache_cmp, cache_sel, win_buf, page_table,
                norm, w_in, pool_w, pe, w1, w2, w_out):
    n_batch, seq, bs, ts, past = shapes
    m_prompt = n_batch * seq
    q_cols = N_HEADS * HEAD_DIM
    proj_cols = q_cols + 3 * KV_COLS
    h = _rmsnorm(x, norm, BF16)
    w_in_b = w_in.astype(BF16)
    proj = _matmul(h, w_in_b, n_cols=proj_cols)
    gates = _matmul(h, _gate_weight(w_in))

    ha, hb = _halfsums_prompt(proj, pool_w, m_prompt, q_cols // KV_COLS)
    n_chunks = seq // CMP_STRIDE
    ckv_p = _cmp_mlp(ha.reshape(n_batch, n_chunks, KV_COLS), hb.reshape(n_batch, n_chunks, KV_COLS),
                     pool_w, pe, w1, w2)
    mix_p = _nsa_prompt(proj, gates, ckv_p, n_batch, seq)

    sa, sb = _halfsums_paged(cache_cmp, layer, page_table, pool_w)
    ckv_s = _cmp_mlp(sa, sb, pool_w, pe, w1, w2)
    o_c, selmask = _nsa_sample_cmp(proj, ckv_s, m_prompt, bs, ts, past)
    mix_s = _nsa_sample_attn(proj, gates, cache_sel, layer, page_table, selmask, win_buf, o_c,
                             m_prompt, bs, ts, past)

    mixed = jnp.concatenate([mix_p, mix_s.astype(BF16)], axis=0)
    x = _matmul_residual(mixed, w_out.astype(BF16), x)

    kv = proj[:, q_cols:].reshape(-1, 3, N_KV_HEADS, 2, HEAD_DIM)
    kv_p = kv[:m_prompt].reshape(n_batch, seq, 3, N_KV_HEADS, 2, HEAD_DIM)
    kv_s = kv[m_prompt:].reshape(bs, ts, 3, N_KV_HEADS, 2, HEAD_DIM)
    n_buf = win_buf.shape[1]
    win_s = jnp.concatenate([win_buf.reshape(bs, n_buf, N_KV_HEADS, 2, HEAD_DIM), kv_s[:, :, 2]], axis=1)
    new = (kv_p[:, :, 0], kv_p[:, :, 1], kv_p[:, -min(WINDOW, seq):, 2],
           kv_s[:, :, 0], kv_s[:, :, 1], win_s[:, -n_buf:])
    return x, new


def _conv_layer(x, shapes, state, norm, w_in, conv_k, w_out):
    n_batch, seq, bs, ts, _ = shapes
    m_prompt = n_batch * seq
    d = w_out.shape[0]
    h = _rmsnorm(x, norm, BF16)
    bgate, u = _matmul_conv_in(h, w_in.astype(BF16))
    k_pad = jnp.pad(conv_k, ((0, 8 - CONV_W), (0, 0)))
    hist_p = jnp.zeros((n_batch, 8, d), F32)
    hist_s = jnp.pad(state, ((0, 0), (8 - (CONV_W - 1), 0), (0, 0)))
    z_p = _conv_apply(u, bgate, hist_p, k_pad, 0, n_batch, seq, BF16)
    z_s = _conv_apply(u, bgate, hist_s, k_pad, m_prompt, bs, ts, F32)
    z = jnp.concatenate([z_p, z_s.astype(BF16)], axis=0)
    x = _matmul_residual(z, w_out.astype(BF16), x)
    new_p = u[:m_prompt].reshape(n_batch, seq, d)[:, -(CONV_W - 1):]
    new_s = u[m_prompt:].reshape(bs, ts, d)[:, -(CONV_W - 1):]
    return x, new_p, new_s


def _ffn(x, norm, w_in, w_out):
    h = _rmsnorm(x, norm, BF16)
    hid = _matmul_swiglu(h, w_in.astype(BF16))
    return _matmul_residual(hid, w_out.astype(BF16), x)


def kernel(x_prompt, x_sample, cache_cmp_kv, cache_sel_kv, state_win_kv, state_conv, page_table, attn_norm, attn_w_in, attn_cmp_pool, attn_cmp_pe, attn_cmp_w1, attn_cmp_w2, attn_w_out, conv_norm, conv_w_in, conv_kernel, conv_w_out, ffn_norm, ffn_w_in, ffn_w_out, final_norm):
    n_batch, seq, d = x_prompt.shape
    bs, ts, _ = x_sample.shape
    past = page_table.shape[1] * PAGE_SIZE
    n_buf = state_win_kv.shape[2]
    depth = ffn_norm.shape[0]
    assert d == N_HEADS * HEAD_DIM and seq % SEL_CHUNK == 0 and seq >= WIN_KEYS
    assert past % PAGE_SIZE == 0 and ts == 8 and (n_batch * seq) % ts == 0
    shapes = (n_batch, seq, bs, ts, past)
    m_prompt = n_batch * seq
    n_attn = cache_cmp_kv.shape[0]
    cache_cmp = cache_cmp_kv.reshape(n_attn, -1, PAGE_SIZE, KV_COLS)
    cache_sel = cache_sel_kv.reshape(n_attn, -1, PAGE_SIZE, KV_COLS)

    x = jnp.concatenate([x_prompt.reshape(m_prompt, d), x_sample.reshape(bs * ts, d)], axis=0)
    attn_new, conv_new = [], []
    for i in range(depth):
        l = i // 2
        if i % 2 == 0:
            win_buf = state_win_kv[l].reshape(bs, n_buf, KV_COLS)
            x, new = _attn_layer(x, l, shapes, cache_cmp, cache_sel, win_buf, page_table,
                                 attn_norm[l], attn_w_in[l], attn_cmp_pool[l], attn_cmp_pe[l],
                                 attn_cmp_w1[l], attn_cmp_w2[l], attn_w_out[l])
            attn_new.append(new)
        else:
            x, new_p, new_s = _conv_layer(x, shapes, state_conv[l], conv_norm[l], conv_w_in[l],
                                          conv_kernel[l], conv_w_out[l])
            conv_new.append((new_p, new_s))
        x = _ffn(x, ffn_norm[i], ffn_w_in[i], ffn_w_out[i])
    y = _rmsnorm(x, final_norm, F32)
    stack = lambda k: jnp.stack([a[k] for a in attn_new])
    return (y[:m_prompt].reshape(n_batch, seq, d), y[m_prompt:].reshape(bs, ts, d),
            stack(0), stack(1), stack(2), jnp.stack([c[0] for c in conv_new]),
            stack(3), stack(4), stack(5), jnp.stack([c[1] for c in conv_new]))
```

```python
import functools

import jax
import jax.numpy as jnp
from jax import lax
from jax.experimental import pallas as pl
from jax.experimental.pallas import tpu as pltpu

N_HEADS = 16
HEAD_DIM = 128
N_KV_HEADS = 4
GROUP = N_HEADS // N_KV_HEADS
KV_SLOTS = N_KV_HEADS * 2
KV_COLS = KV_SLOTS * HEAD_DIM
CMP_BLOCK = 32
CMP_STRIDE = 16
SEL_BLOCK = 64
SEL_SHIFT = 6
TOP_N = 16
N_LOCAL_FORCED = 2
WINDOW = 512
Q_BLOCK = 64
PAGE_SIZE = 128
CONV_W = 3
EPS = 1e-6
SCALE = HEAD_DIM ** -0.5
NEG = -1e30

LANES = 128
SUBLANES = 8
SEL_CHUNK = 512
SEL_SPAN = 1024
WIN_KEYS = WINDOW + 2 * Q_BLOCK
PAGES_PER_STEP = 8
VMEM_LIMIT = 56 * 1024 * 1024

F32 = jnp.float32
BF16 = jnp.bfloat16


def _params(n_axes):
    return pltpu.CompilerParams(dimension_semantics=("arbitrary",) * n_axes,
                                vmem_limit_bytes=VMEM_LIMIT)


def _pick_tile(n, limit, mult):
    best = None
    for t in range(mult, min(n, limit) + 1, mult):
        if n % t == 0:
            best = t
    assert best is not None, (n, limit, mult)
    return best


def _dot(a, b):
    return jnp.dot(a, b, preferred_element_type=F32)


def _dot_nt(a, b):
    return lax.dot_general(a, b, (((1,), (1,)), ((), ())), preferred_element_type=F32)


def _silu(x):
    return x * jax.nn.sigmoid(x)


def _slot_rows(ref, slot, n_rows):
    return ref[pl.ds(slot, n_rows, stride=KV_SLOTS), :]


def _rmsnorm_kernel(x_ref, g_ref, o_ref):
    x = x_ref[...]
    ms = jnp.mean(x * x, axis=-1, keepdims=True)
    o_ref[...] = ((x * lax.rsqrt(ms + EPS)) * g_ref[...]).astype(o_ref.dtype)


def _rmsnorm(x, gain, out_dtype):
    m, d = x.shape
    tm = _pick_tile(m, 1024, 16)
    return pl.pallas_call(
        _rmsnorm_kernel,
        grid=(m // tm,),
        in_specs=[pl.BlockSpec((tm, d), lambda i: (i, 0)),
                  pl.BlockSpec((1, d), lambda i: (0, 0))],
        out_specs=pl.BlockSpec((tm, d), lambda i: (i, 0)),
        out_shape=jax.ShapeDtypeStruct((m, d), out_dtype),
        compiler_params=_params(1),
    )(x, gain.reshape(1, d))


def _mm_kernel(a_ref, b_ref, o_ref, *, scale):
    acc = _dot(a_ref[...], b_ref[...])
    o_ref[...] = acc if scale is None else acc * scale


def _matmul(a, b, n_cols=None, scale=None):
    m, k = a.shape
    n = b.shape[1] if n_cols is None else n_cols
    tm = _pick_tile(m, 1024, 16)
    tn = _pick_tile(n, 512, LANES)
    return pl.pallas_call(
        functools.partial(_mm_kernel, scale=scale),
        grid=(n // tn, m // tm),
        in_specs=[pl.BlockSpec((tm, k), lambda j, i: (i, 0)),
                  pl.BlockSpec((k, tn), lambda j, i: (0, j))],
        out_specs=pl.BlockSpec((tm, tn), lambda j, i: (i, j)),
        out_shape=jax.ShapeDtypeStruct((m, n), F32),
        compiler_params=_params(2),
    )(a, b)


def _mm_kv_kernel(a_ref, b_ref, rows_ref, heads_ref):
    acc = _dot(a_ref[...], b_ref[...])
    tm = acc.shape[0]
    for c in range(KV_SLOTS):
        part = acc[:, c * HEAD_DIM:(c + 1) * HEAD_DIM]
        rows_ref[pl.ds(c, tm, stride=KV_SLOTS), :] = part
        heads_ref[c] = part.astype(heads_ref.dtype)


def _matmul_kv(a, w, col0):
    m, k = a.shape
    tm = _pick_tile(m, 1024, 16)
    blk0 = col0 // KV_COLS
    return pl.pallas_call(
        _mm_kv_kernel,
        grid=(3, m // tm),
        in_specs=[pl.BlockSpec((tm, k), lambda j, i: (i, 0)),
                  pl.BlockSpec((k, KV_COLS), lambda j, i: (0, blk0 + j))],
        out_specs=[pl.BlockSpec((None, tm * KV_SLOTS, HEAD_DIM), lambda j, i: (j, i, 0)),
                   pl.BlockSpec((None, KV_SLOTS, tm, HEAD_DIM), lambda j, i: (j, 0, i, 0))],
        out_shape=[jax.ShapeDtypeStruct((3, m * KV_SLOTS, HEAD_DIM), F32),
                   jax.ShapeDtypeStruct((3, KV_SLOTS, m, HEAD_DIM), BF16)],
        compiler_params=_params(2),
    )(a, w)


def _mm_res_kernel(a_ref, b_ref, r_ref, o_ref):
    kk = pl.program_id(2)
    part = _dot(a_ref[...], b_ref[...])

    @pl.when(kk == 0)
    def _():
        o_ref[...] = r_ref[...] + part

    @pl.when(kk > 0)
    def _():
        o_ref[...] += part


def _matmul_residual(a, b, res):
    m, k = a.shape
    n = b.shape[1]
    tm = _pick_tile(m, 1024, 16)
    tn = _pick_tile(n, 512, LANES)
    tk = _pick_tile(k, 3072, LANES)
    return pl.pallas_call(
        _mm_res_kernel,
        grid=(n // tn, m // tm, k // tk),
        in_specs=[pl.BlockSpec((tm, tk), lambda j, i, kk: (i, kk)),
                  pl.BlockSpec((tk, tn), lambda j, i, kk: (kk, j)),
                  pl.BlockSpec((tm, tn), lambda j, i, kk: (i, j))],
        out_specs=pl.BlockSpec((tm, tn), lambda j, i, kk: (i, j)),
        out_shape=jax.ShapeDtypeStruct((m, n), F32),
        compiler_params=_params(3),
    )(a, b, res)


def _mm_swiglu_kernel(a_ref, bg_ref, bu_ref, o_ref):
    a = a_ref[...]
    gate = _dot(a, bg_ref[...])
    up = _dot(a, bu_ref[...])
    o_ref[...] = (_silu(gate) * up).astype(o_ref.dtype)


def _matmul_swiglu(a, w):
    m, k = a.shape
    f = w.shape[1] // 2
    tm = _pick_tile(m, 1024, 16)
    tn = _pick_tile(f, 512, LANES)
    nb = f // tn
    return pl.pallas_call(
        _mm_swiglu_kernel,
        grid=(nb, m // tm),
        in_specs=[pl.BlockSpec((tm, k), lambda j, i: (i, 0)),
                  pl.BlockSpec((k, tn), lambda j, i: (0, j)),
                  pl.BlockSpec((k, tn), lambda j, i: (0, j + nb))],
        out_specs=pl.BlockSpec((tm, tn), lambda j, i: (i, j)),
        out_shape=jax.ShapeDtypeStruct((m, f), BF16),
        compiler_params=_params(2),
    )(a, w, w)


def _mm_conv_in_kernel(a_ref, wb_ref, wc_ref, wx_ref, b_ref, u_ref):
    a = a_ref[...]
    b_ref[...] = _dot(a, wb_ref[...])
    u_ref[...] = _dot(a, wc_ref[...]) * _dot(a, wx_ref[...])


def _matmul_conv_in(a, w):
    m, k = a.shape
    dc = w.shape[1] // 3
    tm = _pick_tile(m, 1024, 16)
    tn = _pick_tile(dc, 512, LANES)
    nb = dc // tn
    out = jax.ShapeDtypeStruct((m, dc), F32)
    return pl.pallas_call(
        _mm_conv_in_kernel,
        grid=(nb, m // tm),
        in_specs=[pl.BlockSpec((tm, k), lambda j, i: (i, 0)),
                  pl.BlockSpec((k, tn), lambda j, i: (0, j)),
                  pl.BlockSpec((k, tn), lambda j, i: (0, j + nb)),
                  pl.BlockSpec((k, tn), lambda j, i: (0, j + 2 * nb))],
        out_specs=[pl.BlockSpec((tm, tn), lambda j, i: (i, j)),
                   pl.BlockSpec((tm, tn), lambda j, i: (i, j))],
        out_shape=[out, out],
        compiler_params=_params(2),
    )(a, w, w, w)


def _conv_apply_kernel(u_ref, b_ref, prev_ref, hist_ref, k_ref, o_ref):
    i = pl.program_id(1)
    u = u_ref[...]
    rows = u.shape[0]
    tail = jnp.where(i == 0, hist_ref[...], prev_ref[...])
    row = lax.broadcasted_iota(jnp.int32, u.shape, 0)
    u1 = jnp.where(row == 0, tail[7:8], pltpu.roll(u, 1, axis=0))
    u2 = jnp.where(row == 0, tail[6:7], jnp.where(row == 1, tail[7:8], pltpu.roll(u, 2 % rows, axis=0)))
    k = k_ref[...]
    v = k[0:1] * u2 + k[1:2] * u1 + k[2:3] * u
    o_ref[...] = (b_ref[...] * v).astype(o_ref.dtype)


def _conv_apply(u, bgate, hist, conv_k, row0, n_seq, seq_len, out_dtype):
    d = u.shape[1]
    tt = _pick_tile(seq_len, 512, SUBLANES)
    tc = _pick_tile(d, 1024, LANES)
    tiles = seq_len // tt
    blk0 = row0 // tt
    sub = tt // SUBLANES

    def cur(s, i, j):
        return (blk0 + s * tiles + i, j)

    def prev(s, i, j):
        return (jnp.maximum((blk0 + s * tiles + i) * sub - 1, 0), j)

    return pl.pallas_call(
        _conv_apply_kernel,
        grid=(n_seq, tiles, d // tc),
        in_specs=[pl.BlockSpec((tt, tc), cur),
                  pl.BlockSpec((tt, tc), cur),
                  pl.BlockSpec((SUBLANES, tc), prev),
                  pl.BlockSpec((None, SUBLANES, tc), lambda s, i, j: (s, 0, j)),
                  pl.BlockSpec((SUBLANES, tc), lambda s, i, j: (0, j))],
        out_specs=pl.BlockSpec((tt, tc), lambda s, i, j: (s * tiles + i, j)),
        out_shape=jax.ShapeDtypeStruct((n_seq * seq_len, d), out_dtype),
        compiler_params=_params(3),
    )(u, bgate, u, hist, conv_k)


def _chunk_halfsums(x, wa, wb):
    n = x.shape[0] // (CMP_STRIDE * KV_SLOTS)
    x4 = x.reshape(n, CMP_STRIDE, KV_SLOTS, HEAD_DIM)
    a = jnp.sum(x4 * wa[None], axis=1).reshape(n * KV_SLOTS, HEAD_DIM)
    b = jnp.sum(x4 * wb[None], axis=1).reshape(n * KV_SLOTS, HEAD_DIM)
    return a, b


def _halfsum_kernel(x_ref, wa_ref, wb_ref, a_ref, b_ref):
    a_ref[...], b_ref[...] = _chunk_halfsums(x_ref[...], wa_ref[...], wb_ref[...])


def _halfsum_paged_kernel(pt_ref, *refs):
    del pt_ref
    page_refs = refs[:PAGES_PER_STEP]
    wa_ref, wb_ref, a_ref, b_ref = refs[PAGES_PER_STEP:]
    rows = (PAGE_SIZE // CMP_STRIDE) * KV_SLOTS
    for k, page_ref in enumerate(page_refs):
        a, b = _chunk_halfsums(page_ref[...], wa_ref[...], wb_ref[...])
        a_ref[k * rows:(k + 1) * rows, :] = a
        b_ref[k * rows:(k + 1) * rows, :] = b


def _pool_weight_maps(pool_w):
    w = pool_w.reshape(2, CMP_BLOCK // CMP_STRIDE, CMP_STRIDE)
    maps = []
    for half in range(2):
        wj = jnp.tile(w[:, half, :].T, (1, N_KV_HEADS))
        maps.append(jnp.broadcast_to(wj[:, :, None], (CMP_STRIDE, KV_SLOTS, HEAD_DIM)))
    return maps


def _halfsums_prompt(kv_rows, pool_w, n_rows):
    wa, wb = _pool_weight_maps(pool_w)
    tr = _pick_tile(n_rows, 512, LANES)
    out_rows = tr // CMP_STRIDE * KV_SLOTS
    out = jax.ShapeDtypeStruct((n_rows // CMP_STRIDE * KV_SLOTS, HEAD_DIM), F32)
    wspec = pl.BlockSpec((CMP_STRIDE, KV_SLOTS, HEAD_DIM), lambda i: (0, 0, 0))
    ospec = pl.BlockSpec((out_rows, HEAD_DIM), lambda i: (i, 0))
    return pl.pallas_call(
        _halfsum_kernel,
        grid=(n_rows // tr,),
        in_specs=[pl.BlockSpec((None, tr * KV_SLOTS, HEAD_DIM), lambda i: (0, i, 0)), wspec, wspec],
        out_specs=[ospec, ospec],
        out_shape=[out, out],
        compiler_params=_params(1),
    )(kv_rows, wa, wb)


def _page_specs(layer, n_pool, n_pages):
    def spec(k):
        def index(b, p, pt):
            page = jnp.minimum(p * PAGES_PER_STEP + k, n_pages - 1)
            return (layer * n_pool + pt[b, page], 0, 0)
        return pl.BlockSpec((None, PAGE_SIZE * KV_SLOTS, HEAD_DIM), index)
    return [spec(k) for k in range(PAGES_PER_STEP)]


def _halfsums_paged(cache, layer, n_pool, page_table, pool_w):
    bs, n_pages = page_table.shape
    assert n_pages % PAGES_PER_STEP == 0
    wa, wb = _pool_weight_maps(pool_w)
    rows = PAGES_PER_STEP * (PAGE_SIZE // CMP_STRIDE) * KV_SLOTS
    out = jax.ShapeDtypeStruct((bs, n_pages * (PAGE_SIZE // CMP_STRIDE) * KV_SLOTS, HEAD_DIM), F32)
    wspec = pl.BlockSpec((CMP_STRIDE, KV_SLOTS, HEAD_DIM), lambda b, p, pt: (0, 0, 0))
    ospec = pl.BlockSpec((None, rows, HEAD_DIM), lambda b, p, pt: (b, p, 0))
    grid_spec = pltpu.PrefetchScalarGridSpec(
        num_scalar_prefetch=1,
        grid=(bs, n_pages // PAGES_PER_STEP),
        in_specs=_page_specs(layer, n_pool, n_pages) + [wspec, wspec],
        out_specs=[ospec, ospec],
    )
    return pl.pallas_call(
        _halfsum_paged_kernel,
        grid_spec=grid_spec,
        out_shape=[out, out],
        compiler_params=_params(2),
    )(page_table, *([cache] * PAGES_PER_STEP), wa, wb)


def _cmp_mlp_kernel(a_ref, b_ref, pw_ref, pe_ref, w1_ref, w2_ref, o_ref):
    n = o_ref.shape[1]
    for k in range(2):
        pe_term = jnp.sum(pw_ref[k] * pe_ref[k], axis=0, keepdims=True)
        w1 = w1_ref[k].astype(BF16)
        w2 = w2_ref[k].astype(BF16)
        for g in range(N_KV_HEADS):
            c = 2 * g + k
            x = _slot_rows(a_ref, c, n) + pltpu.roll(_slot_rows(b_ref, c, n), n - 1, axis=0) + pe_term
            hid = _silu(_dot(x.astype(BF16), w1))
            o_ref[c] = _dot(hid.astype(BF16), w2)


def _cmp_mlp(half_a, half_b, pool_w, pe, w1, w2):
    n_seq, rows, _ = half_a.shape
    n_chunks = rows // KV_SLOTS
    spec = pl.BlockSpec((None, rows, HEAD_DIM), lambda b: (b, 0, 0))
    full3 = lambda shape: pl.BlockSpec(shape, lambda b: (0, 0, 0))
    return pl.pallas_call(
        _cmp_mlp_kernel,
        grid=(n_seq,),
        in_specs=[spec, spec, full3((2, CMP_BLOCK, 1)), full3((2, CMP_BLOCK, HEAD_DIM)),
                  full3((2, HEAD_DIM, HEAD_DIM)), full3((2, HEAD_DIM, HEAD_DIM))],
        out_specs=pl.BlockSpec((None, KV_SLOTS, n_chunks, HEAD_DIM), lambda b: (b, 0, 0, 0)),
        out_shape=jax.ShapeDtypeStruct((n_seq, KV_SLOTS, n_chunks, HEAD_DIM), F32),
        compiler_params=_params(1),
    )(half_a, half_b, pool_w.reshape(2, CMP_BLOCK, 1), pe, w1, w2)


def _overlap_matrix(n_cmp_pad, n_sel_pad):
    c_start = jnp.arange(n_cmp_pad)[:, None] * CMP_STRIDE
    b_start = jnp.arange(n_sel_pad)[None, :] * SEL_BLOCK
    ov = (c_start < b_start + SEL_BLOCK) & (c_start + CMP_BLOCK > b_start)
    return ov.astype(BF16)


def _stack_heads(q):
    return jnp.concatenate([q[:, r * HEAD_DIM:(r + 1) * HEAD_DIM] for r in range(GROUP)], axis=0)


def _cmp_attend(q4, kc, vc, tpos):
    t = tpos.shape[0]
    n_cmp = kc.shape[0]
    s = _dot_nt(q4, kc).reshape(GROUP, t, n_cmp)
    last = lax.broadcasted_iota(jnp.int32, (1, n_cmp), 1) * CMP_STRIDE + (CMP_BLOCK - 1)
    ok = (last <= tpos)[None]
    sm = jnp.where(ok, s, NEG)
    m = jnp.max(sm, axis=-1, keepdims=True)
    e = jnp.where(ok, jnp.exp(sm - m), 0.0)
    p = e / jnp.maximum(jnp.sum(e, axis=-1, keepdims=True), 1e-30)
    pb = p.astype(BF16).reshape(GROUP * t, n_cmp)
    o_c = _dot(pb, vc).reshape(GROUP, t, HEAD_DIM)
    return o_c, pb


def _block_scores(imp, blk, tpos, n_sel):
    cur = tpos >> SEL_SHIFT
    real = blk < n_sel
    valid = (blk * SEL_BLOCK <= tpos) & real
    forced = ((blk == 0) | ((blk <= cur) & (blk > cur - N_LOCAL_FORCED))) & real
    return jnp.where(forced, jnp.inf, jnp.where(valid, imp, -jnp.inf))


def _masked_online_step(q4, k, v, bias, carry):
    m_i, l_i, acc = carry
    t, n_keys = bias.shape
    s = _dot_nt(q4, k).reshape(GROUP, t, n_keys) + bias[None]
    m_new = jnp.maximum(m_i, jnp.max(s, axis=-1, keepdims=True))
    alpha = jnp.exp(m_i - m_new)
    p = jnp.exp(s - m_new)
    l_new = alpha * l_i + jnp.sum(p, axis=-1, keepdims=True)
    pv = _dot(p.astype(BF16).reshape(GROUP * t, n_keys), v).reshape(GROUP, t, HEAD_DIM)
    return m_new, l_new, alpha * acc + pv


def _online_init(t):
    return (jnp.full((GROUP, t, 1), NEG, F32), jnp.zeros((GROUP, t, 1), F32),
            jnp.zeros((GROUP, t, HEAD_DIM), F32))


def _fold_lane_tiles(x, op):
    out = x[..., 0:LANES]
    for j in range(1, x.shape[-1] // LANES):
        out = op(out, x[..., j * LANES:(j + 1) * LANES])
    return out


def _gated_mix(gate, lane0, o_c, o_s, o_w):
    outs = []
    for r in range(GROUP):
        outs.append(gate[:, lane0 + r:lane0 + r + 1] * o_c[r]
                    + gate[:, lane0 + GROUP + r:lane0 + GROUP + r + 1] * o_s[r]
                    + gate[:, lane0 + 2 * GROUP + r:lane0 + 2 * GROUP + r + 1] * o_w[r])
    return outs


def _topk_mask_by_rank(score_t, k, n_live):
    n, lanes = score_t.shape
    tiles = [score_t[v * SUBLANES:(v + 1) * SUBLANES] for v in range(n_live // SUBLANES)]
    ranks = [jnp.zeros((SUBLANES, lanes), F32) for _ in tiles]
    row = lax.broadcasted_iota(jnp.int32, (SUBLANES, 1), 0)
    for i in range(n_live):
        si = score_t[i:i + 1, :]
        for v, tile in enumerate(tiles):
            lo = v * SUBLANES
            if lo > i:
                ahead = jnp.where(si >= tile, 1.0, 0.0)
            elif lo + SUBLANES - 1 < i:
                ahead = jnp.where(si > tile, 1.0, 0.0)
            else:
                tie = jnp.where(row + lo > i, 1.0, 0.0)
                ahead = jnp.where(si > tile, 1.0, jnp.where(si == tile, tie, 0.0))
            ranks[v] = ranks[v] + ahead
    live = [jnp.where(tile > -jnp.inf, jnp.where(rank < k, 1.0, 0.0), 0.0) for tile, rank in zip(tiles, ranks)]
    dead = [jnp.zeros((n - n_live, lanes), F32)] if n > n_live else []
    return jnp.concatenate(live + dead, axis=0)


def _nsa_prompt_kernel(q_ref, ks_ref, vs_ref, kw_ref, vw_ref, kc_ref, vc_ref, gate_ref, ov_ref, exp_ref,
                       o_ref, os_scr, s_scr):
    c = pl.program_id(2)
    tq = Q_BLOCK
    seq = ks_ref.shape[0]
    n_sel = ov_ref.shape[0]
    start = c * tq
    q4 = _stack_heads(q_ref[...]).astype(BF16)
    tpos = start + lax.broadcasted_iota(jnp.int32, (tq, 1), 0)

    o_c, pb = _cmp_attend(q4, kc_ref[...].astype(BF16), vc_ref[...].astype(BF16), tpos)

    imp_t = _fold_lane_tiles(_dot_nt(ov_ref[...], pb), jnp.add)
    imp_t = imp_t + pltpu.roll(imp_t, tq, axis=1)
    tpos_t = start + (lax.broadcasted_iota(jnp.int32, (1, LANES), 1) & (tq - 1))
    blk_t = lax.broadcasted_iota(jnp.int32, (n_sel, 1), 0)
    score_t = _block_scores(imp_t, blk_t, tpos_t, n_sel)

    qb_per_span = SEL_SPAN // tq
    for v in range(seq // SEL_SPAN):
        @pl.when((c >= v * qb_per_span) & (c < (v + 1) * qb_per_span))
        def _():
            n_keys = (v + 1) * SEL_SPAN
            sel_t = _topk_mask_by_rank(score_t, min(TOP_N, n_sel), n_keys // SEL_BLOCK)
            sel_sq = jnp.concatenate([sel_t, jnp.zeros((LANES - n_sel, LANES), F32)], axis=0)
            selb = sel_sq.T[0:tq].astype(BF16)
            m_lane = jnp.full((GROUP, tq, LANES), NEG, F32)
            for kc_i in range(n_keys // SEL_CHUNK):
                off = kc_i * SEL_CHUNK
                picked = _dot(selb, exp_ref[kc_i])
                kpos = off + lax.broadcasted_iota(jnp.int32, (1, SEL_CHUNK), 1)
                bias = jnp.where(picked > 0.5, jnp.where(kpos <= tpos, 0.0, NEG), NEG)
                s_i = _dot_nt(q4, ks_ref[off:off + SEL_CHUNK, :]).reshape(GROUP, tq, SEL_CHUNK) + bias[None]
                s_scr[:, :, off:off + SEL_CHUNK] = s_i
                m_lane = jnp.maximum(m_lane, _fold_lane_tiles(s_i, jnp.maximum))
            m_s = jnp.max(m_lane, axis=-1, keepdims=True)
            l_lane = jnp.zeros((GROUP, tq, LANES), F32)
            acc = jnp.zeros((GROUP * tq, HEAD_DIM), F32)
            for kc_i in range(n_keys // SEL_CHUNK):
                off = kc_i * SEL_CHUNK
                p_i = jnp.exp(s_scr[:, :, off:off + SEL_CHUNK] - m_s)
                l_lane = l_lane + _fold_lane_tiles(p_i, jnp.add)
                acc = acc + _dot(p_i.astype(BF16).reshape(GROUP * tq, SEL_CHUNK), vs_ref[off:off + SEL_CHUNK, :])
            l_s = jnp.sum(l_lane, axis=-1, keepdims=True)
            os_scr[...] = acc.reshape(GROUP, tq, HEAD_DIM) / l_s

    lo = pl.multiple_of(jnp.maximum(start + tq - WIN_KEYS, 0), Q_BLOCK)
    rel = tpos - (lo + lax.broadcasted_iota(jnp.int32, (1, WIN_KEYS), 1))
    bias_w = jnp.where((rel >= 0) & (rel < WINDOW), 0.0, NEG)
    _, l_w, acc_w = _masked_online_step(q4, kw_ref[pl.ds(lo, WIN_KEYS), :], vw_ref[pl.ds(lo, WIN_KEYS), :],
                                        bias_w, _online_init(tq))
    o_w = acc_w / l_w

    gate = jax.nn.sigmoid(gate_ref[...])
    o_ref[...] = jnp.concatenate(_gated_mix(gate, 0, o_c, os_scr[...], o_w), axis=1).astype(o_ref.dtype)


def _nsa_prompt(q, kv_heads, gates, ckv, n_batch, seq):
    n_qb = seq // Q_BLOCK
    n_cmp = seq // CMP_STRIDE
    n_sel = seq // SEL_BLOCK
    n_chunks = seq // SEL_CHUNK
    qw = GROUP * HEAD_DIM
    assert GROUP * Q_BLOCK == 2 * LANES and n_sel <= LANES and n_sel % SUBLANES == 0
    overlap_t = _overlap_matrix(n_cmp, n_sel).T
    key_blk = jnp.arange(seq) // SEL_BLOCK
    expand = (jnp.arange(LANES)[:, None] == key_blk[None, :]).astype(BF16)
    expand = expand.reshape(LANES, n_chunks, SEL_CHUNK).transpose(1, 0, 2)

    def kv_spec(branch, kv):
        return pl.BlockSpec((None, None, seq, HEAD_DIM), lambda b, g, c: (branch, 2 * g + kv, b, 0))

    def ckv_spec(kv):
        return pl.BlockSpec((None, None, n_cmp, HEAD_DIM), lambda b, g, c: (b, 2 * g + kv, 0, 0))

    return pl.pallas_call(
        _nsa_prompt_kernel,
        grid=(n_batch, N_KV_HEADS, n_qb),
        in_specs=[pl.BlockSpec((Q_BLOCK, qw), lambda b, g, c: (b * n_qb + c, g)),
                  kv_spec(1, 0), kv_spec(1, 1), kv_spec(2, 0), kv_spec(2, 1),
                  ckv_spec(0), ckv_spec(1),
                  pl.BlockSpec((Q_BLOCK, LANES), lambda b, g, c: (b * n_qb + c, g)),
                  pl.BlockSpec((n_sel, n_cmp), lambda b, g, c: (0, 0)),
                  pl.BlockSpec((n_chunks, LANES, SEL_CHUNK), lambda b, g, c: (0, 0, 0))],
        out_specs=pl.BlockSpec((Q_BLOCK, qw), lambda b, g, c: (b * n_qb + c, g)),
        out_shape=jax.ShapeDtypeStruct((n_batch * seq, N_HEADS * HEAD_DIM), BF16),
        scratch_shapes=[pltpu.VMEM((GROUP, Q_BLOCK, HEAD_DIM), F32),
                        pltpu.VMEM((GROUP, Q_BLOCK, seq), F32)],
        compiler_params=_params(3),
    )(q, kv_heads, kv_heads, kv_heads, kv_heads, ckv, ckv, gates, overlap_t, expand)


def _topk_mask_by_extraction(score, k):
    t, n = score.shape
    col = lax.broadcasted_iota(jnp.int32, (t, n), 1).astype(F32)
    sel = jnp.zeros((t, n), F32)
    for _ in range(k):
        m = jnp.max(score, axis=-1, keepdims=True)
        first = jnp.min(jnp.where(score == m, col, float(n)), axis=-1, keepdims=True)
        pick = (col == first) & (m > -jnp.inf)
        sel = jnp.where(pick, 1.0, sel)
        score = jnp.where(col == first, -jnp.inf, score)
    return sel


def _nsa_sample_cmp_kernel(q_ref, kc_ref, vc_ref, ov_ref, oc_ref, sel_ref, *, past, n_sel):
    ts = q_ref.shape[0]
    q4 = _stack_heads(q_ref[...]).astype(BF16)
    tpos = past + lax.broadcasted_iota(jnp.int32, (ts, 1), 0)
    o_c, pb = _cmp_attend(q4, kc_ref[...].astype(BF16), vc_ref[...].astype(BF16), tpos)
    oc_ref[...] = jnp.concatenate([o_c[r] for r in range(GROUP)], axis=1)
    n_sel_pad = ov_ref.shape[1]
    imp = jnp.sum(_dot(pb, ov_ref[...]).reshape(GROUP, ts, n_sel_pad), axis=0)
    blk = lax.broadcasted_iota(jnp.int32, (ts, n_sel_pad), 1)
    sel_ref[...] = _topk_mask_by_extraction(_block_scores(imp, blk, tpos, n_sel), min(TOP_N, n_sel))


def _nsa_sample_cmp(q, ckv, row0, bs, ts, past):
    n_cmp = ckv.shape[2]
    n_sel = -(-(past + ts) // SEL_BLOCK)
    n_sel_pad = -(-n_sel // LANES) * LANES
    qw = GROUP * HEAD_DIM
    overlap = _overlap_matrix(n_cmp, n_sel_pad)
    kern = functools.partial(_nsa_sample_cmp_kernel, past=past, n_sel=n_sel)

    def ckv_spec(kv):
        return pl.BlockSpec((None, None, n_cmp, HEAD_DIM), lambda b, g: (b, 2 * g + kv, 0, 0))

    return pl.pallas_call(
        kern,
        grid=(bs, N_KV_HEADS),
        in_specs=[pl.BlockSpec((ts, qw), lambda b, g: (row0 // ts + b, g)),
                  ckv_spec(0), ckv_spec(1),
                  pl.BlockSpec((n_cmp, n_sel_pad), lambda b, g: (0, 0))],
        out_specs=[pl.BlockSpec((ts, qw), lambda b, g: (b, g)),
                   pl.BlockSpec((None, None, ts, n_sel_pad), lambda b, g: (b, g, 0, 0))],
        out_shape=[jax.ShapeDtypeStruct((bs * ts, N_HEADS * HEAD_DIM), F32),
                   jax.ShapeDtypeStruct((bs, N_KV_HEADS, ts, n_sel_pad), F32)],
        compiler_params=_params(2),
    )(q, ckv, ckv, overlap)


def _nsa_sample_attn_kernel(pt_ref, *refs, past, n_buf, n_pages):
    del pt_ref
    page_refs = refs[:PAGES_PER_STEP]
    (q_ref, newsel_ref, selmask_ref, winbuf_ref, newwin_ref, oc_ref, gate_ref,
     o_ref, m_scr, l_scr, acc_scr) = refs[PAGES_PER_STEP:]
    step = pl.program_id(1)
    n_steps = n_pages // PAGES_PER_STEP
    ts = q_ref.shape[0]
    n_sel_pad = selmask_ref.shape[2]
    tpos = past + lax.broadcasted_iota(jnp.int32, (ts, 1), 0)
    q = q_ref[...]

    def q_heads(g):
        return _stack_heads(q[:, g * GROUP * HEAD_DIM:(g + 1) * GROUP * HEAD_DIM]).astype(BF16)

    @pl.when(step == 0)
    def _():
        m_scr[...] = jnp.full(m_scr.shape, NEG, F32)
        l_scr[...] = jnp.zeros(l_scr.shape, F32)
        acc_scr[...] = jnp.zeros(acc_scr.shape, F32)

    def sel_update(kv_refs, first_page):
        n_keys = len(kv_refs) * PAGE_SIZE
        blk = lax.broadcasted_iota(jnp.int32, (n_sel_pad, n_keys), 0)
        key = lax.broadcasted_iota(jnp.int32, (n_sel_pad, n_keys), 1)
        first_blk = first_page * (PAGE_SIZE // SEL_BLOCK)
        expand = jnp.where(blk == first_blk + (key >> SEL_SHIFT), 1.0, 0.0).astype(BF16)
        kpos = first_page * PAGE_SIZE + lax.broadcasted_iota(jnp.int32, (1, n_keys), 1)
        for g in range(N_KV_HEADS):
            k = jnp.concatenate([_slot_rows(r, 2 * g, PAGE_SIZE) for r in kv_refs], axis=0).astype(BF16)
            v = jnp.concatenate([_slot_rows(r, 2 * g + 1, PAGE_SIZE) for r in kv_refs], axis=0).astype(BF16)
            picked = _dot(selmask_ref[g].astype(BF16), expand)
            bias = jnp.where((picked > 0.5) & (kpos <= tpos), 0.0, NEG)
            m_scr[g], l_scr[g], acc_scr[g] = _masked_online_step(
                q_heads(g), k, v, bias, (m_scr[g], l_scr[g], acc_scr[g]))

    @pl.when(step < n_steps)
    def _():
        sel_update(page_refs, step * PAGES_PER_STEP)

    @pl.when(step == n_steps)
    def _():
        sel_update([newsel_ref], n_pages)
        gate = jax.nn.sigmoid(gate_ref[...])
        rel_buf = tpos - (past - n_buf + lax.broadcasted_iota(jnp.int32, (1, n_buf), 1))
        bias_buf = jnp.where((rel_buf >= 0) & (rel_buf < WINDOW), 0.0, NEG)
        rel_new = tpos - (past + lax.broadcasted_iota(jnp.int32, (1, PAGE_SIZE), 1))
        bias_new = jnp.where((rel_new >= 0) & (rel_new < WINDOW), 0.0, NEG)
        outs = []
        for g in range(N_KV_HEADS):
            qg = q_heads(g)
            o_s = acc_scr[g] / l_scr[g]
            carry = _masked_online_step(qg, _slot_rows(winbuf_ref, 2 * g, n_buf).astype(BF16),
                                        _slot_rows(winbuf_ref, 2 * g + 1, n_buf).astype(BF16),
                                        bias_buf, _online_init(ts))
            _, l_w, acc_w = _masked_online_step(qg, _slot_rows(newwin_ref, 2 * g, PAGE_SIZE).astype(BF16),
                                                _slot_rows(newwin_ref, 2 * g + 1, PAGE_SIZE).astype(BF16),
                                                bias_new, carry)
            o_w = acc_w / l_w
            h0 = g * GROUP * HEAD_DIM
            o_c = [oc_ref[:, h0 + r * HEAD_DIM:h0 + (r + 1) * HEAD_DIM] for r in range(GROUP)]
            outs += _gated_mix(gate, g * LANES, o_c, o_s, o_w)
        o_ref[...] = jnp.concatenate(outs, axis=1)


def _nsa_sample_attn(q, kv_rows, gates, cache_sel, layer, n_pool, page_table, selmask, win_buf, o_c,
                     row0, bs, ts, past):
    n_pages = page_table.shape[1]
    assert n_pages % PAGES_PER_STEP == 0
    n_buf = win_buf.shape[1] // KV_SLOTS
    n_sel_pad = selmask.shape[3]
    q_cols = N_HEADS * HEAD_DIM
    rb = row0 // ts
    page_rows = PAGE_SIZE * KV_SLOTS
    pad = ((0, 0), (0, (PAGE_SIZE - ts) * KV_SLOTS), (0, 0))
    new_sel = jnp.pad(kv_rows[1, row0 * KV_SLOTS:].reshape(bs, ts * KV_SLOTS, HEAD_DIM), pad)
    new_win = jnp.pad(kv_rows[2, row0 * KV_SLOTS:].reshape(bs, ts * KV_SLOTS, HEAD_DIM), pad)
    kern = functools.partial(_nsa_sample_attn_kernel, past=past, n_buf=n_buf, n_pages=n_pages)
    per_seq = lambda rows: pl.BlockSpec((None, rows, HEAD_DIM), lambda b, p, pt: (b, 0, 0))
    grid_spec = pltpu.PrefetchScalarGridSpec(
        num_scalar_prefetch=1,
        grid=(bs, n_pages // PAGES_PER_STEP + 1),
        in_specs=_page_specs(layer, n_pool, n_pages) + [
            pl.BlockSpec((ts, q_cols), lambda b, p, pt: (rb + b, 0)),
            per_seq(page_rows),
            pl.BlockSpec((None, N_KV_HEADS, ts, n_sel_pad), lambda b, p, pt: (b, 0, 0, 0)),
            per_seq(n_buf * KV_SLOTS),
            per_seq(page_rows),
            pl.BlockSpec((ts, q_cols), lambda b, p, pt: (b, 0)),
            pl.BlockSpec((ts, N_KV_HEADS * LANES), lambda b, p, pt: (rb + b, 0))],
        out_specs=pl.BlockSpec((ts, q_cols), lambda b, p, pt: (b, 0)),
        scratch_shapes=[pltpu.VMEM((N_KV_HEADS, GROUP, ts, 1), F32),
                        pltpu.VMEM((N_KV_HEADS, GROUP, ts, 1), F32),
                        pltpu.VMEM((N_KV_HEADS, GROUP, ts, HEAD_DIM), F32)],
    )
    return pl.pallas_call(
        kern,
        grid_spec=grid_spec,
        out_shape=jax.ShapeDtypeStruct((bs * ts, q_cols), F32),
        compiler_params=_params(2),
    )(page_table, *([cache_sel] * PAGES_PER_STEP), q, new_sel, selmask, win_buf, new_win, o_c, gates)


def _gate_weight(w_in):
    q_cols = N_HEADS * HEAD_DIM
    wg = w_in[:, q_cols + 3 * KV_COLS:].reshape(-1, 3, N_KV_HEADS, GROUP).transpose(0, 2, 1, 3)
    wg = wg.reshape(-1, N_KV_HEADS, 3 * GROUP)
    wg = jnp.pad(wg, ((0, 0), (0, 0), (0, LANES - 3 * GROUP)))
    return wg.reshape(-1, N_KV_HEADS * LANES).astype(BF16)


def _attn_layer(x, layer, shapes, cache_cmp, cache_sel, n_pool, win_buf, page_table,
                norm, w_in, pool_w, pe, w1, w2, w_out):
    n_batch, seq, bs, ts, past = shapes
    m_prompt = n_batch * seq
    q_cols = N_HEADS * HEAD_DIM
    h = _rmsnorm(x, norm, BF16)
    w_in_b = w_in.astype(BF16)
    q = _matmul(h, w_in_b, n_cols=q_cols, scale=SCALE)
    kv_rows, kv_heads = _matmul_kv(h, w_in_b, q_cols)
    gates = _matmul(h, _gate_weight(w_in))

    ha, hb = _halfsums_prompt(kv_rows, pool_w, m_prompt)
    half_rows = seq // CMP_STRIDE * KV_SLOTS
    ckv_p = _cmp_mlp(ha.reshape(n_batch, half_rows, HEAD_DIM), hb.reshape(n_batch, half_rows, HEAD_DIM),
                     pool_w, pe, w1, w2)
    mix_p = _nsa_prompt(q, kv_heads, gates, ckv_p, n_batch, seq)

    sa, sb = _halfsums_paged(cache_cmp, layer, n_pool, page_table, pool_w)
    ckv_s = _cmp_mlp(sa, sb, pool_w, pe, w1, w2)
    o_c, selmask = _nsa_sample_cmp(q, ckv_s, m_prompt, bs, ts, past)
    mix_s = _nsa_sample_attn(q, kv_rows, gates, cache_sel, layer, n_pool, page_table, selmask, win_buf, o_c,
                             m_prompt, bs, ts, past)

    mixed = jnp.concatenate([mix_p, mix_s.astype(BF16)], axis=0)
    x = _matmul_residual(mixed, w_out.astype(BF16), x)

    kv_p = kv_rows[:, :m_prompt * KV_SLOTS].reshape(3, n_batch, seq, N_KV_HEADS, 2, HEAD_DIM)
    kv_s = kv_rows[:, m_prompt * KV_SLOTS:].reshape(3, bs, ts, N_KV_HEADS, 2, HEAD_DIM)
    n_buf = win_buf.shape[1] // KV_SLOTS
    win_s = jnp.concatenate([win_buf.reshape(bs, n_buf, N_KV_HEADS, 2, HEAD_DIM), kv_s[2]], axis=1)
    new = (kv_p[0], kv_p[1], kv_p[2, :, -min(WINDOW, seq):], kv_s[0], kv_s[1], win_s[:, -n_buf:])
    return x, new


def _conv_layer(x, shapes, state, norm, w_in, conv_k, w_out):
    n_batch, seq, bs, ts, _ = shapes
    m_prompt = n_batch * seq
    d = w_out.shape[0]
    h = _rmsnorm(x, norm, BF16)
    bgate, u = _matmul_conv_in(h, w_in.astype(BF16))
    k_pad = jnp.pad(conv_k, ((0, SUBLANES - CONV_W), (0, 0)))
    hist_p = jnp.zeros((n_batch, SUBLANES, d), F32)
    hist_s = jnp.pad(state, ((0, 0), (SUBLANES - (CONV_W - 1), 0), (0, 0)))
    z_p = _conv_apply(u, bgate, hist_p, k_pad, 0, n_batch, seq, BF16)
    z_s = _conv_apply(u, bgate, hist_s, k_pad, m_prompt, bs, ts, F32)
    z = jnp.concatenate([z_p, z_s.astype(BF16)], axis=0)
    x = _matmul_residual(z, w_out.astype(BF16), x)
    new_p = u[:m_prompt].reshape(n_batch, seq, d)[:, -(CONV_W - 1):]
    new_s = u[m_prompt:].reshape(bs, ts, d)[:, -(CONV_W - 1):]
    return x, new_p, new_s


def _ffn(x, norm, w_in, w_out):
    h = _rmsnorm(x, norm, BF16)
    hid = _matmul_swiglu(h, w_in.astype(BF16))
    return _matmul_residual(hid, w_out.astype(BF16), x)


def kernel(x_prompt, x_sample, cache_cmp_kv, cache_sel_kv, state_win_kv, state_conv, page_table, attn_norm, attn_w_in, attn_cmp_pool, attn_cmp_pe, attn_cmp_w1, attn_cmp_w2, attn_w_out, conv_norm, conv_w_in, conv_kernel, conv_w_out, ffn_norm, ffn_w_in, ffn_w_out, final_norm):
    n_batch, seq, d = x_prompt.shape
    bs, ts, _ = x_sample.shape
    past = page_table.shape[1] * PAGE_SIZE
    n_buf = state_win_kv.shape[2]
    depth = ffn_norm.shape[0]
    assert d == N_HEADS * HEAD_DIM and seq % SEL_SPAN == 0 and seq >= WIN_KEYS
    assert ts == SUBLANES and (n_batch * seq) % ts == 0
    shapes = (n_batch, seq, bs, ts, past)
    m_prompt = n_batch * seq
    n_attn, n_pool = cache_cmp_kv.shape[:2]
    cache_cmp = cache_cmp_kv.reshape(n_attn * n_pool, PAGE_SIZE * KV_SLOTS, HEAD_DIM)
    cache_sel = cache_sel_kv.reshape(n_attn * n_pool, PAGE_SIZE * KV_SLOTS, HEAD_DIM)

    x = jnp.concatenate([x_prompt.reshape(m_prompt, d), x_sample.reshape(bs * ts, d)], axis=0)
    attn_new, conv_new = [], []
    for i in range(depth):
        l = i // 2
        if i % 2 == 0:
            win_buf = state_win_kv[l].reshape(bs, n_buf * KV_SLOTS, HEAD_DIM)
            x, new = _attn_layer(x, l, shapes, cache_cmp, cache_sel, n_pool, win_buf, page_table,
                                 attn_norm[l], attn_w_in[l], attn_cmp_pool[l], attn_cmp_pe[l],
                                 attn_cmp_w1[l], attn_cmp_w2[l], attn_w_out[l])
            attn_new.append(new)
        else:
            x, new_p, new_s = _conv_layer(x, shapes, state_conv[l], conv_norm[l], conv_w_in[l],
                                          conv_kernel[l], conv_w_out[l])
            conv_new.append((new_p, new_s))
        x = _ffn(x, ffn_norm[i], ffn_w_in[i], ffn_w_out[i])
    y = _rmsnorm(x, final_norm, F32)
    stack = lambda k: jnp.stack([a[k] for a in attn_new])
    return (y[:m_prompt].reshape(n_batch, seq, d), y[m_prompt:].reshape(bs, ts, d),
            stack(0), stack(1), stack(2), jnp.stack([c[0] for c in conv_new]),
            stack(3), stack(4), stack(5), jnp.stack([c[1] for c in conv_new]))
```

```python
import functools

import jax
import jax.numpy as jnp
from jax import lax
from jax.experimental import pallas as pl
from jax.experimental.pallas import tpu as pltpu

N_HEADS = 16
HEAD_DIM = 128
N_KV_HEADS = 4
GROUP = N_HEADS // N_KV_HEADS
KV_SLOTS = N_KV_HEADS * 2
KV_COLS = KV_SLOTS * HEAD_DIM
CMP_BLOCK = 32
CMP_STRIDE = 16
SEL_BLOCK = 64
SEL_SHIFT = 6
TOP_N = 16
N_LOCAL_FORCED = 2
WINDOW = 512
PAGE_SIZE = 128
CONV_W = 3
EPS = 1e-6
SCALE = HEAD_DIM ** -0.5
NEG = -1e30

LANES = 128
SUBLANES = 8
SEL_CHUNK = 512
SEL_SPAN = 1024
Q_TILE = 64
WIN_KEYS = WINDOW + LANES
PAGES_PER_STEP = 8
VMEM_LIMIT = 56 * 1024 * 1024

F32 = jnp.float32
BF16 = jnp.bfloat16


def _params(n_axes):
    return pltpu.CompilerParams(dimension_semantics=("arbitrary",) * n_axes,
                                vmem_limit_bytes=VMEM_LIMIT)


def _pick_tile(n, limit, mult):
    best = None
    for t in range(mult, min(n, limit) + 1, mult):
        if n % t == 0:
            best = t
    assert best is not None, (n, limit, mult)
    return best


def _dot(a, b):
    return jnp.dot(a, b, preferred_element_type=F32)


def _dot_nt(a, b):
    return lax.dot_general(a, b, (((1,), (1,)), ((), ())), preferred_element_type=F32)


def _silu(x):
    return x * jax.nn.sigmoid(x)


def _slot_rows(ref, slot, n_rows):
    return ref[pl.ds(slot, n_rows, stride=KV_SLOTS), :]


def _rmsnorm_kernel(x_ref, g_ref, o_ref):
    x = x_ref[...]
    ms = jnp.mean(x * x, axis=-1, keepdims=True)
    o_ref[...] = ((x * lax.rsqrt(ms + EPS)) * g_ref[...]).astype(o_ref.dtype)


def _rmsnorm(x, gain, out_dtype):
    m, d = x.shape
    tm = _pick_tile(m, 1024, 16)
    return pl.pallas_call(
        _rmsnorm_kernel,
        grid=(m // tm,),
        in_specs=[pl.BlockSpec((tm, d), lambda i: (i, 0)),
                  pl.BlockSpec((1, d), lambda i: (0, 0))],
        out_specs=pl.BlockSpec((tm, d), lambda i: (i, 0)),
        out_shape=jax.ShapeDtypeStruct((m, d), out_dtype),
        compiler_params=_params(1),
    )(x, gain.reshape(1, d))


def _cast_weights_once(w_refs, w_scrs):
    @pl.when(pl.program_id(1) == 0)
    def _():
        for w_ref, w_scr in zip(w_refs, w_scrs):
            w_scr[...] = w_ref[...].astype(BF16)


def _w_spec(k, tn, layer, blk0=0):
    return pl.BlockSpec((None, k, tn), lambda j, i: (layer, 0, blk0 + j))


def _w_scratch(k, tn, count=1):
    return [pltpu.VMEM((k, tn), BF16) for _ in range(count)]


def _mm_kernel(a_ref, w_ref, o_ref, w_scr, *, scale):
    _cast_weights_once([w_ref], [w_scr])
    acc = _dot(a_ref[...], w_scr[...])
    o_ref[...] = acc if scale is None else acc * scale


def _matmul(a, w, layer, n, scale=None):
    m, k = a.shape
    tm = _pick_tile(m, 1024, 16)
    tn = _pick_tile(n, 512, LANES)
    return pl.pallas_call(
        functools.partial(_mm_kernel, scale=scale),
        grid=(n // tn, m // tm),
        in_specs=[pl.BlockSpec((tm, k), lambda j, i: (i, 0)), _w_spec(k, tn, layer)],
        out_specs=pl.BlockSpec((tm, tn), lambda j, i: (i, j)),
        out_shape=jax.ShapeDtypeStruct((m, n), F32),
        scratch_shapes=_w_scratch(k, tn),
        compiler_params=_params(2),
    )(a, w)


def _mm_kv_kernel(a_ref, w_ref, rows_ref, heads_ref, w_scr):
    _cast_weights_once([w_ref], [w_scr])
    acc = _dot(a_ref[...], w_scr[...])
    tm = acc.shape[0]
    for c in range(KV_SLOTS):
        part = acc[:, c * HEAD_DIM:(c + 1) * HEAD_DIM]
        rows_ref[pl.ds(c, tm, stride=KV_SLOTS), :] = part
        heads_ref[c] = part.astype(heads_ref.dtype)


def _matmul_kv(a, w, layer, col0):
    m, k = a.shape
    tm = _pick_tile(m, 1024, 16)
    return pl.pallas_call(
        _mm_kv_kernel,
        grid=(3, m // tm),
        in_specs=[pl.BlockSpec((tm, k), lambda j, i: (i, 0)), _w_spec(k, KV_COLS, layer, col0 // KV_COLS)],
        out_specs=[pl.BlockSpec((None, tm * KV_SLOTS, HEAD_DIM), lambda j, i: (j, i, 0)),
                   pl.BlockSpec((None, KV_SLOTS, tm, HEAD_DIM), lambda j, i: (j, 0, i, 0))],
        out_shape=[jax.ShapeDtypeStruct((3, m * KV_SLOTS, HEAD_DIM), F32),
                   jax.ShapeDtypeStruct((3, KV_SLOTS, m, HEAD_DIM), BF16)],
        scratch_shapes=_w_scratch(k, KV_COLS),
        compiler_params=_params(2),
    )(a, w)


def _mm_res_kernel(a_ref, w_ref, r_ref, o_ref, w_scr):
    _cast_weights_once([w_ref], [w_scr])
    o_ref[...] = r_ref[...] + _dot(a_ref[...], w_scr[...])


def _matmul_residual(a, w, layer, res):
    m, k = a.shape
    n = w.shape[2]
    tm = _pick_tile(m, 1024, 16)
    tn = max(t for t in (LANES, 2 * LANES, 4 * LANES) if n % t == 0 and (t == LANES or 8 * k * t <= VMEM_LIMIT // 4))
    return pl.pallas_call(
        _mm_res_kernel,
        grid=(n // tn, m // tm),
        in_specs=[pl.BlockSpec((tm, k), lambda j, i: (i, 0)), _w_spec(k, tn, layer),
                  pl.BlockSpec((tm, tn), lambda j, i: (i, j))],
        out_specs=pl.BlockSpec((tm, tn), lambda j, i: (i, j)),
        out_shape=jax.ShapeDtypeStruct((m, n), F32),
        scratch_shapes=_w_scratch(k, tn),
        compiler_params=_params(2),
    )(a, w, res)


def _mm_swiglu_kernel(a_ref, wg_ref, wu_ref, o_ref, wg_scr, wu_scr):
    _cast_weights_once([wg_ref, wu_ref], [wg_scr, wu_scr])
    a = a_ref[...]
    gate = _dot(a, wg_scr[...])
    up = _dot(a, wu_scr[...])
    o_ref[...] = (_silu(gate) * up).astype(o_ref.dtype)


def _matmul_swiglu(a, w, layer):
    m, k = a.shape
    f = w.shape[2] // 2
    tm = _pick_tile(m, 1024, 16)
    tn = _pick_tile(f, 512, LANES)
    nb = f // tn
    return pl.pallas_call(
        _mm_swiglu_kernel,
        grid=(nb, m // tm),
        in_specs=[pl.BlockSpec((tm, k), lambda j, i: (i, 0)), _w_spec(k, tn, layer), _w_spec(k, tn, layer, nb)],
        out_specs=pl.BlockSpec((tm, tn), lambda j, i: (i, j)),
        out_shape=jax.ShapeDtypeStruct((m, f), BF16),
        scratch_shapes=_w_scratch(k, tn, 2),
        compiler_params=_params(2),
    )(a, w, w)


def _mm_conv_in_kernel(a_ref, wb_ref, wc_ref, wx_ref, b_ref, u_ref, wb_scr, wc_scr, wx_scr):
    _cast_weights_once([wb_ref, wc_ref, wx_ref], [wb_scr, wc_scr, wx_scr])
    a = a_ref[...]
    b_ref[...] = _dot(a, wb_scr[...])
    u_ref[...] = _dot(a, wc_scr[...]) * _dot(a, wx_scr[...])


def _matmul_conv_in(a, w, layer):
    m, k = a.shape
    dc = w.shape[2] // 3
    tm = _pick_tile(m, 1024, 16)
    tn = _pick_tile(dc, 512, LANES)
    nb = dc // tn
    out = jax.ShapeDtypeStruct((m, dc), F32)
    return pl.pallas_call(
        _mm_conv_in_kernel,
        grid=(nb, m // tm),
        in_specs=[pl.BlockSpec((tm, k), lambda j, i: (i, 0)),
                  _w_spec(k, tn, layer), _w_spec(k, tn, layer, nb), _w_spec(k, tn, layer, 2 * nb)],
        out_specs=[pl.BlockSpec((tm, tn), lambda j, i: (i, j)),
                   pl.BlockSpec((tm, tn), lambda j, i: (i, j))],
        out_shape=[out, out],
        scratch_shapes=_w_scratch(k, tn, 3),
        compiler_params=_params(2),
    )(a, w, w, w)


def _conv_apply_kernel(u_ref, b_ref, prev_ref, hist_ref, k_ref, o_ref):
    i = pl.program_id(1)
    u = u_ref[...]
    rows = u.shape[0]
    tail = jnp.where(i == 0, hist_ref[...], prev_ref[...])
    row = lax.broadcasted_iota(jnp.int32, u.shape, 0)
    u1 = jnp.where(row == 0, tail[7:8], pltpu.roll(u, 1, axis=0))
    u2 = jnp.where(row == 0, tail[6:7], jnp.where(row == 1, tail[7:8], pltpu.roll(u, 2 % rows, axis=0)))
    k = k_ref[...]
    v = k[0:1] * u2 + k[1:2] * u1 + k[2:3] * u
    o_ref[...] = (b_ref[...] * v).astype(o_ref.dtype)


def _conv_apply(u, bgate, hist, conv_k, row0, n_seq, seq_len, out_dtype):
    d = u.shape[1]
    tt = _pick_tile(seq_len, 512, SUBLANES)
    tc = _pick_tile(d, 1024, LANES)
    tiles = seq_len // tt
    blk0 = row0 // tt
    sub = tt // SUBLANES

    def cur(s, i, j):
        return (blk0 + s * tiles + i, j)

    def prev(s, i, j):
        return (jnp.maximum((blk0 + s * tiles + i) * sub - 1, 0), j)

    return pl.pallas_call(
        _conv_apply_kernel,
        grid=(n_seq, tiles, d // tc),
        in_specs=[pl.BlockSpec((tt, tc), cur),
                  pl.BlockSpec((tt, tc), cur),
                  pl.BlockSpec((SUBLANES, tc), prev),
                  pl.BlockSpec((None, SUBLANES, tc), lambda s, i, j: (s, 0, j)),
                  pl.BlockSpec((SUBLANES, tc), lambda s, i, j: (0, j))],
        out_specs=pl.BlockSpec((tt, tc), lambda s, i, j: (s * tiles + i, j)),
        out_shape=jax.ShapeDtypeStruct((n_seq * seq_len, d), out_dtype),
        compiler_params=_params(3),
    )(u, bgate, u, hist, conv_k)


def _chunk_halfsums(x, wa, wb):
    n = x.shape[0] // (CMP_STRIDE * KV_SLOTS)
    x4 = x.reshape(n, CMP_STRIDE, KV_SLOTS, HEAD_DIM)
    a = jnp.sum(x4 * wa[None], axis=1).reshape(n * KV_SLOTS, HEAD_DIM)
    b = jnp.sum(x4 * wb[None], axis=1).reshape(n * KV_SLOTS, HEAD_DIM)
    return a, b


def _halfsum_kernel(x_ref, wa_ref, wb_ref, a_ref, b_ref):
    a_ref[...], b_ref[...] = _chunk_halfsums(x_ref[...], wa_ref[...], wb_ref[...])


def _halfsum_paged_kernel(pt_ref, *refs):
    del pt_ref
    page_refs = refs[:PAGES_PER_STEP]
    wa_ref, wb_ref, a_ref, b_ref = refs[PAGES_PER_STEP:]
    rows = (PAGE_SIZE // CMP_STRIDE) * KV_SLOTS
    for k, page_ref in enumerate(page_refs):
        a, b = _chunk_halfsums(page_ref[...], wa_ref[...], wb_ref[...])
        a_ref[k * rows:(k + 1) * rows, :] = a
        b_ref[k * rows:(k + 1) * rows, :] = b


def _pool_weight_maps(pool_w):
    w = pool_w.reshape(2, CMP_BLOCK // CMP_STRIDE, CMP_STRIDE)
    maps = []
    for half in range(2):
        wj = jnp.tile(w[:, half, :].T, (1, N_KV_HEADS))
        maps.append(jnp.broadcast_to(wj[:, :, None], (CMP_STRIDE, KV_SLOTS, HEAD_DIM)))
    return maps


def _halfsums_prompt(kv_rows, pool_w, n_rows):
    wa, wb = _pool_weight_maps(pool_w)
    tr = _pick_tile(n_rows, 512, LANES)
    out_rows = tr // CMP_STRIDE * KV_SLOTS
    out = jax.ShapeDtypeStruct((n_rows // CMP_STRIDE * KV_SLOTS, HEAD_DIM), F32)
    wspec = pl.BlockSpec((CMP_STRIDE, KV_SLOTS, HEAD_DIM), lambda i: (0, 0, 0))
    ospec = pl.BlockSpec((out_rows, HEAD_DIM), lambda i: (i, 0))
    return pl.pallas_call(
        _halfsum_kernel,
        grid=(n_rows // tr,),
        in_specs=[pl.BlockSpec((None, tr * KV_SLOTS, HEAD_DIM), lambda i: (0, i, 0)), wspec, wspec],
        out_specs=[ospec, ospec],
        out_shape=[out, out],
        compiler_params=_params(1),
    )(kv_rows, wa, wb)


def _page_specs(layer, n_pool, n_pages):
    def spec(k):
        def index(b, p, pt):
            page = jnp.minimum(p * PAGES_PER_STEP + k, n_pages - 1)
            return (layer * n_pool + pt[b, page], 0, 0)
        return pl.BlockSpec((None, PAGE_SIZE * KV_SLOTS, HEAD_DIM), index)
    return [spec(k) for k in range(PAGES_PER_STEP)]


def _halfsums_paged(cache, layer, n_pool, page_table, pool_w):
    bs, n_pages = page_table.shape
    assert n_pages % PAGES_PER_STEP == 0
    wa, wb = _pool_weight_maps(pool_w)
    rows = PAGES_PER_STEP * (PAGE_SIZE // CMP_STRIDE) * KV_SLOTS
    out = jax.ShapeDtypeStruct((bs, n_pages * (PAGE_SIZE // CMP_STRIDE) * KV_SLOTS, HEAD_DIM), F32)
    wspec = pl.BlockSpec((CMP_STRIDE, KV_SLOTS, HEAD_DIM), lambda b, p, pt: (0, 0, 0))
    ospec = pl.BlockSpec((None, rows, HEAD_DIM), lambda b, p, pt: (b, p, 0))
    grid_spec = pltpu.PrefetchScalarGridSpec(
        num_scalar_prefetch=1,
        grid=(bs, n_pages // PAGES_PER_STEP),
        in_specs=_page_specs(layer, n_pool, n_pages) + [wspec, wspec],
        out_specs=[ospec, ospec],
    )
    return pl.pallas_call(
        _halfsum_paged_kernel,
        grid_spec=grid_spec,
        out_shape=[out, out],
        compiler_params=_params(2),
    )(page_table, *([cache] * PAGES_PER_STEP), wa, wb)


def _cmp_mlp_kernel(a_ref, b_ref, pw_ref, pe_ref, w1_ref, w2_ref, o_ref):
    n = o_ref.shape[1]
    for k in range(2):
        pe_term = jnp.sum(pw_ref[k] * pe_ref[k], axis=0, keepdims=True)
        w1 = w1_ref[k].astype(BF16)
        w2 = w2_ref[k].astype(BF16)
        for g in range(N_KV_HEADS):
            c = 2 * g + k
            x = _slot_rows(a_ref, c, n) + pltpu.roll(_slot_rows(b_ref, c, n), n - 1, axis=0) + pe_term
            hid = _silu(_dot(x.astype(BF16), w1))
            o_ref[c] = _dot(hid.astype(BF16), w2)


def _cmp_mlp(half_a, half_b, pool_w, pe, w1, w2):
    n_seq, rows, _ = half_a.shape
    n_chunks = rows // KV_SLOTS
    spec = pl.BlockSpec((None, rows, HEAD_DIM), lambda b: (b, 0, 0))
    full3 = lambda shape: pl.BlockSpec(shape, lambda b: (0, 0, 0))
    return pl.pallas_call(
        _cmp_mlp_kernel,
        grid=(n_seq,),
        in_specs=[spec, spec, full3((2, CMP_BLOCK, 1)), full3((2, CMP_BLOCK, HEAD_DIM)),
                  full3((2, HEAD_DIM, HEAD_DIM)), full3((2, HEAD_DIM, HEAD_DIM))],
        out_specs=pl.BlockSpec((None, KV_SLOTS, n_chunks, HEAD_DIM), lambda b: (b, 0, 0, 0)),
        out_shape=jax.ShapeDtypeStruct((n_seq, KV_SLOTS, n_chunks, HEAD_DIM), F32),
        compiler_params=_params(1),
    )(half_a, half_b, pool_w.reshape(2, CMP_BLOCK, 1), pe, w1, w2)


def _overlap_matrix(n_cmp_pad, n_sel_pad):
    c_start = jnp.arange(n_cmp_pad)[:, None] * CMP_STRIDE
    b_start = jnp.arange(n_sel_pad)[None, :] * SEL_BLOCK
    ov = (c_start < b_start + SEL_BLOCK) & (c_start + CMP_BLOCK > b_start)
    return ov.astype(BF16)


def _stack_heads(q):
    return jnp.concatenate([q[:, r * HEAD_DIM:(r + 1) * HEAD_DIM] for r in range(GROUP)], axis=0)


def _cmp_attend(q4, kc, vc, tpos):
    t = tpos.shape[0]
    n_cmp = kc.shape[0]
    s = _dot_nt(q4, kc).reshape(GROUP, t, n_cmp)
    last = lax.broadcasted_iota(jnp.int32, (1, n_cmp), 1) * CMP_STRIDE + (CMP_BLOCK - 1)
    ok = (last <= tpos)[None]
    sm = jnp.where(ok, s, NEG)
    m = jnp.max(sm, axis=-1, keepdims=True)
    e = jnp.where(ok, jnp.exp(sm - m), 0.0)
    p = e / jnp.maximum(jnp.sum(e, axis=-1, keepdims=True), 1e-30)
    pb = p.astype(BF16).reshape(GROUP * t, n_cmp)
    o_c = _dot(pb, vc).reshape(GROUP, t, HEAD_DIM)
    return o_c, pb


def _block_scores(imp, blk, tpos, n_sel):
    cur = tpos >> SEL_SHIFT
    real = blk < n_sel
    valid = (blk * SEL_BLOCK <= tpos) & real
    forced = ((blk == 0) | ((blk <= cur) & (blk > cur - N_LOCAL_FORCED))) & real
    return jnp.where(forced, jnp.inf, jnp.where(valid, imp, -jnp.inf))


def _masked_online_step(q4, k, v, bias, carry):
    m_i, l_i, acc = carry
    t, n_keys = bias.shape
    s = _dot_nt(q4, k).reshape(GROUP, t, n_keys) + bias[None]
    m_new = jnp.maximum(m_i, jnp.max(s, axis=-1, keepdims=True))
    alpha = jnp.exp(m_i - m_new)
    p = jnp.exp(s - m_new)
    l_new = alpha * l_i + jnp.sum(p, axis=-1, keepdims=True)
    pv = _dot(p.astype(BF16).reshape(GROUP * t, n_keys), v).reshape(GROUP, t, HEAD_DIM)
    return m_new, l_new, alpha * acc + pv


def _online_init(t):
    return (jnp.full((GROUP, t, 1), NEG, F32), jnp.zeros((GROUP, t, 1), F32),
            jnp.zeros((GROUP, t, HEAD_DIM), F32))


def _fold_lane_tiles(x, op):
    out = x[..., 0:LANES]
    for j in range(1, x.shape[-1] // LANES):
        out = op(out, x[..., j * LANES:(j + 1) * LANES])
    return out


def _gated_mix(gate, lane0, o_c, o_s, o_w):
    outs = []
    for r in range(GROUP):
        lanes = [lane0 + branch * N_HEADS + r for branch in range(3)]
        outs.append(gate[:, lanes[0]:lanes[0] + 1] * o_c[r]
                    + gate[:, lanes[1]:lanes[1] + 1] * o_s[r]
                    + gate[:, lanes[2]:lanes[2] + 1] * o_w[r])
    return outs


def _topk_mask_by_rank(score_t, k, n_live):
    n, lanes = score_t.shape
    tiles = [score_t[v * SUBLANES:(v + 1) * SUBLANES] for v in range(n_live // SUBLANES)]
    ranks = [jnp.zeros((SUBLANES, lanes), F32) for _ in tiles]
    row = lax.broadcasted_iota(jnp.int32, (SUBLANES, 1), 0)
    for i in range(n_live):
        si = score_t[i:i + 1, :]
        for v, tile in enumerate(tiles):
            lo = v * SUBLANES
            if lo > i:
                ahead = jnp.where(si >= tile, 1.0, 0.0)
            elif lo + SUBLANES - 1 < i:
                ahead = jnp.where(si > tile, 1.0, 0.0)
            else:
                tie = jnp.where(row + lo > i, 1.0, 0.0)
                ahead = jnp.where(si > tile, 1.0, jnp.where(si == tile, tie, 0.0))
            ranks[v] = ranks[v] + ahead
    live = [jnp.where(tile > -jnp.inf, jnp.where(rank < k, 1.0, 0.0), 0.0) for tile, rank in zip(tiles, ranks)]
    dead = [jnp.zeros((n - n_live, lanes), F32)] if n > n_live else []
    return jnp.concatenate(live + dead, axis=0)


def _nsa_prompt_kernel(q_ref, ks_ref, vs_ref, kw_ref, vw_ref, kc_ref, vc_ref, gate_ref, ov_ref, exp_ref,
                       o_ref, os_scr, s_scr):
    c = pl.program_id(2)
    tq = Q_TILE
    seq = ks_ref.shape[0]
    n_sel = ov_ref.shape[0]
    start = c * tq
    q4 = _stack_heads(q_ref[...]).astype(BF16)
    tpos = start + lax.broadcasted_iota(jnp.int32, (tq, 1), 0)

    o_c, pb = _cmp_attend(q4, kc_ref[...].astype(BF16), vc_ref[...].astype(BF16), tpos)

    imp_t = _fold_lane_tiles(_dot_nt(ov_ref[...], pb), jnp.add)
    imp_t = imp_t + pltpu.roll(imp_t, tq, axis=1)
    tpos_t = start + (lax.broadcasted_iota(jnp.int32, (1, LANES), 1) & (tq - 1))
    blk_t = lax.broadcasted_iota(jnp.int32, (n_sel, 1), 0)
    score_t = _block_scores(imp_t, blk_t, tpos_t, n_sel)

    lo = pl.multiple_of(jnp.maximum(start + tq - WIN_KEYS, 0), tq)
    rel = tpos - (lo + lax.broadcasted_iota(jnp.int32, (1, WIN_KEYS), 1))
    bias_w = jnp.where((rel >= 0) & (rel < WINDOW), 0.0, NEG)
    _, l_w, acc_w = _masked_online_step(q4, kw_ref[pl.ds(lo, WIN_KEYS), :], vw_ref[pl.ds(lo, WIN_KEYS), :],
                                        bias_w, _online_init(tq))
    o_w = acc_w / l_w

    qb_per_span = SEL_SPAN // tq
    for v in range(seq // SEL_SPAN):
        @pl.when((c >= v * qb_per_span) & (c < (v + 1) * qb_per_span))
        def _():
            n_keys = (v + 1) * SEL_SPAN
            sel_t = _topk_mask_by_rank(score_t, min(TOP_N, n_sel), n_keys // SEL_BLOCK)
            sel_sq = jnp.concatenate([sel_t, jnp.zeros((LANES - n_sel, LANES), F32)], axis=0)
            selb = sel_sq.T[0:tq].astype(BF16)
            m_lane = jnp.full((GROUP, tq, LANES), NEG, F32)
            for k in range(n_keys // SEL_CHUNK):
                off = k * SEL_CHUNK
                picked = _dot(selb, exp_ref[k])
                kpos = off + lax.broadcasted_iota(jnp.int32, (1, SEL_CHUNK), 1)
                bias = jnp.where(picked > 0.5, jnp.where(kpos <= tpos, 0.0, NEG), NEG)
                s_k = _dot_nt(q4, ks_ref[off:off + SEL_CHUNK, :]).reshape(GROUP, tq, SEL_CHUNK) + bias[None]
                s_scr[k] = s_k
                m_lane = jnp.maximum(m_lane, _fold_lane_tiles(s_k, jnp.maximum))
            m_s = jnp.max(m_lane, axis=-1, keepdims=True)
            l_lane = jnp.zeros((GROUP, tq, LANES), F32)
            acc = jnp.zeros((GROUP * tq, HEAD_DIM), F32)
            for k in range(n_keys // SEL_CHUNK):
                off = k * SEL_CHUNK
                p_k = jnp.exp(s_scr[k] - m_s)
                l_lane = l_lane + _fold_lane_tiles(p_k, jnp.add)
                acc = acc + _dot(p_k.astype(BF16).reshape(GROUP * tq, SEL_CHUNK), vs_ref[off:off + SEL_CHUNK, :])
            os_scr[...] = acc.reshape(GROUP, tq, HEAD_DIM) / jnp.sum(l_lane, axis=-1, keepdims=True)

    gate = pltpu.roll(jax.nn.sigmoid(gate_ref[...]), LANES - GROUP * pl.program_id(1), axis=1)
    o_ref[...] = jnp.concatenate(_gated_mix(gate, 0, o_c, os_scr[...], o_w), axis=1).astype(o_ref.dtype)


def _nsa_prompt(q, kv_heads, gates, ckv, n_batch, seq):
    n_qb = seq // Q_TILE
    n_cmp = seq // CMP_STRIDE
    n_sel = seq // SEL_BLOCK
    n_chunks = seq // SEL_CHUNK
    qw = GROUP * HEAD_DIM
    assert GROUP * Q_TILE == 2 * LANES and n_sel <= LANES and n_sel % SUBLANES == 0
    overlap_t = _overlap_matrix(n_cmp, n_sel).T
    key_blk = jnp.arange(seq) // SEL_BLOCK
    expand = (jnp.arange(LANES)[:, None] == key_blk[None, :]).astype(BF16)
    expand = expand.reshape(LANES, n_chunks, SEL_CHUNK).transpose(1, 0, 2)

    def kv_spec(branch, kv):
        return pl.BlockSpec((None, None, seq, HEAD_DIM), lambda b, g, c: (branch, 2 * g + kv, b, 0))

    def ckv_spec(kv):
        return pl.BlockSpec((None, None, n_cmp, HEAD_DIM), lambda b, g, c: (b, 2 * g + kv, 0, 0))

    return pl.pallas_call(
        _nsa_prompt_kernel,
        grid=(n_batch, N_KV_HEADS, n_qb),
        in_specs=[pl.BlockSpec((Q_TILE, qw), lambda b, g, c: (b * n_qb + c, g)),
                  kv_spec(1, 0), kv_spec(1, 1), kv_spec(2, 0), kv_spec(2, 1),
                  ckv_spec(0), ckv_spec(1),
                  pl.BlockSpec((Q_TILE, LANES), lambda b, g, c: (b * n_qb + c, 0)),
                  pl.BlockSpec((n_sel, n_cmp), lambda b, g, c: (0, 0)),
                  pl.BlockSpec((n_chunks, LANES, SEL_CHUNK), lambda b, g, c: (0, 0, 0))],
        out_specs=pl.BlockSpec((Q_TILE, qw), lambda b, g, c: (b * n_qb + c, g)),
        out_shape=jax.ShapeDtypeStruct((n_batch * seq, N_HEADS * HEAD_DIM), BF16),
        scratch_shapes=[pltpu.VMEM((GROUP, Q_TILE, HEAD_DIM), F32),
                        pltpu.VMEM((n_chunks, GROUP, Q_TILE, SEL_CHUNK), F32)],
        compiler_params=_params(3),
    )(q, kv_heads, kv_heads, kv_heads, kv_heads, ckv, ckv, gates, overlap_t, expand)


def _topk_mask_by_extraction(score, k):
    t, n = score.shape
    col = lax.broadcasted_iota(jnp.int32, (t, n), 1).astype(F32)
    sel = jnp.zeros((t, n), F32)
    for _ in range(k):
        m = jnp.max(score, axis=-1, keepdims=True)
        first = jnp.min(jnp.where(score == m, col, float(n)), axis=-1, keepdims=True)
        pick = (col == first) & (m > -jnp.inf)
        sel = jnp.where(pick, 1.0, sel)
        score = jnp.where(col == first, -jnp.inf, score)
    return sel


def _nsa_sample_cmp_kernel(q_ref, kc_ref, vc_ref, ov_ref, oc_ref, sel_ref, *, past, n_sel):
    ts = q_ref.shape[0]
    q4 = _stack_heads(q_ref[...]).astype(BF16)
    tpos = past + lax.broadcasted_iota(jnp.int32, (ts, 1), 0)
    o_c, pb = _cmp_attend(q4, kc_ref[...].astype(BF16), vc_ref[...].astype(BF16), tpos)
    oc_ref[...] = jnp.concatenate([o_c[r] for r in range(GROUP)], axis=1)
    n_sel_pad = ov_ref.shape[1]
    imp = jnp.sum(_dot(pb, ov_ref[...]).reshape(GROUP, ts, n_sel_pad), axis=0)
    blk = lax.broadcasted_iota(jnp.int32, (ts, n_sel_pad), 1)
    sel_ref[...] = _topk_mask_by_extraction(_block_scores(imp, blk, tpos, n_sel), min(TOP_N, n_sel))


def _nsa_sample_cmp(q, ckv, row0, bs, ts, past):
    n_cmp = ckv.shape[2]
    n_sel = -(-(past + ts) // SEL_BLOCK)
    n_sel_pad = -(-n_sel // LANES) * LANES
    qw = GROUP * HEAD_DIM
    overlap = _overlap_matrix(n_cmp, n_sel_pad)
    kern = functools.partial(_nsa_sample_cmp_kernel, past=past, n_sel=n_sel)

    def ckv_spec(kv):
        return pl.BlockSpec((None, None, n_cmp, HEAD_DIM), lambda b, g: (b, 2 * g + kv, 0, 0))

    return pl.pallas_call(
        kern,
        grid=(bs, N_KV_HEADS),
        in_specs=[pl.BlockSpec((ts, qw), lambda b, g: (row0 // ts + b, g)),
                  ckv_spec(0), ckv_spec(1),
                  pl.BlockSpec((n_cmp, n_sel_pad), lambda b, g: (0, 0))],
        out_specs=[pl.BlockSpec((ts, qw), lambda b, g: (b, g)),
                   pl.BlockSpec((None, None, ts, n_sel_pad), lambda b, g: (b, g, 0, 0))],
        out_shape=[jax.ShapeDtypeStruct((bs * ts, N_HEADS * HEAD_DIM), F32),
                   jax.ShapeDtypeStruct((bs, N_KV_HEADS, ts, n_sel_pad), F32)],
        compiler_params=_params(2),
    )(q, ckv, ckv, overlap)


def _nsa_sample_attn_kernel(pt_ref, *refs, past, n_buf, n_pages):
    del pt_ref
    page_refs = refs[:PAGES_PER_STEP]
    (q_ref, newsel_ref, selmask_ref, exp_ref, winbuf_ref, newwin_ref, oc_ref, gate_ref,
     o_ref, m_scr, l_scr, acc_scr) = refs[PAGES_PER_STEP:]
    step = pl.program_id(1)
    n_steps = n_pages // PAGES_PER_STEP
    ts = q_ref.shape[0]
    tpos = past + lax.broadcasted_iota(jnp.int32, (ts, 1), 0)
    q = q_ref[...]

    def q_heads(g):
        return _stack_heads(q[:, g * GROUP * HEAD_DIM:(g + 1) * GROUP * HEAD_DIM]).astype(BF16)

    @pl.when(step == 0)
    def _():
        m_scr[...] = jnp.full(m_scr.shape, NEG, F32)
        l_scr[...] = jnp.zeros(l_scr.shape, F32)
        acc_scr[...] = jnp.zeros(acc_scr.shape, F32)

    def sel_update(kv_refs, first_page):
        n_keys = len(kv_refs) * PAGE_SIZE
        expand = exp_ref[:, 0:n_keys]
        kpos = first_page * PAGE_SIZE + lax.broadcasted_iota(jnp.int32, (1, n_keys), 1)
        for g in range(N_KV_HEADS):
            k = jnp.concatenate([_slot_rows(r, 2 * g, PAGE_SIZE) for r in kv_refs], axis=0).astype(BF16)
            v = jnp.concatenate([_slot_rows(r, 2 * g + 1, PAGE_SIZE) for r in kv_refs], axis=0).astype(BF16)
            picked = _dot(selmask_ref[g].astype(BF16), expand)
            bias = jnp.where((picked > 0.5) & (kpos <= tpos), 0.0, NEG)
            m_scr[g], l_scr[g], acc_scr[g] = _masked_online_step(
                q_heads(g), k, v, bias, (m_scr[g], l_scr[g], acc_scr[g]))

    @pl.when(step < n_steps)
    def _():
        sel_update(page_refs, step * PAGES_PER_STEP)

    @pl.when(step == n_steps)
    def _():
        sel_update([newsel_ref], n_pages)
        gate = jax.nn.sigmoid(gate_ref[...])
        rel_buf = tpos - (past - n_buf + lax.broadcasted_iota(jnp.int32, (1, n_buf), 1))
        bias_buf = jnp.where((rel_buf >= 0) & (rel_buf < WINDOW), 0.0, NEG)
        rel_new = tpos - (past + lax.broadcasted_iota(jnp.int32, (1, PAGE_SIZE), 1))
        bias_new = jnp.where((rel_new >= 0) & (rel_new < WINDOW), 0.0, NEG)
        outs = []
        for g in range(N_KV_HEADS):
            qg = q_heads(g)
            o_s = acc_scr[g] / l_scr[g]
            carry = _masked_online_step(qg, _slot_rows(winbuf_ref, 2 * g, n_buf).astype(BF16),
                                        _slot_rows(winbuf_ref, 2 * g + 1, n_buf).astype(BF16),
                                        bias_buf, _online_init(ts))
            _, l_w, acc_w = _masked_online_step(qg, _slot_rows(newwin_ref, 2 * g, PAGE_SIZE).astype(BF16),
                                                _slot_rows(newwin_ref, 2 * g + 1, PAGE_SIZE).astype(BF16),
                                                bias_new, carry)
            o_w = acc_w / l_w
            h0 = g * GROUP * HEAD_DIM
            o_c = [oc_ref[:, h0 + r * HEAD_DIM:h0 + (r + 1) * HEAD_DIM] for r in range(GROUP)]
            outs += _gated_mix(gate, g * GROUP, o_c, o_s, o_w)
        o_ref[...] = jnp.concatenate(outs, axis=1)


def _nsa_sample_attn(q, kv_rows, gates, cache_sel, layer, n_pool, page_table, selmask, win_buf, o_c,
                     row0, bs, ts, past):
    n_pages = page_table.shape[1]
    assert n_pages % PAGES_PER_STEP == 0
    n_buf = win_buf.shape[1] // KV_SLOTS
    n_sel_pad = selmask.shape[3]
    q_cols = N_HEADS * HEAD_DIM
    rb = row0 // ts
    page_rows = PAGE_SIZE * KV_SLOTS
    bps = PAGES_PER_STEP * PAGE_SIZE // SEL_BLOCK
    assert n_sel_pad % bps == 0
    selmask = selmask.reshape(bs, N_KV_HEADS, ts, n_sel_pad // bps, bps).transpose(0, 3, 1, 2, 4)
    key_blk = jnp.arange(PAGES_PER_STEP * PAGE_SIZE) // SEL_BLOCK
    expand = (jnp.arange(bps)[:, None] == key_blk[None, :]).astype(BF16)
    pad = ((0, 0), (0, (PAGE_SIZE - ts) * KV_SLOTS), (0, 0))
    new_sel = jnp.pad(kv_rows[1, row0 * KV_SLOTS:].reshape(bs, ts * KV_SLOTS, HEAD_DIM), pad)
    new_win = jnp.pad(kv_rows[2, row0 * KV_SLOTS:].reshape(bs, ts * KV_SLOTS, HEAD_DIM), pad)
    kern = functools.partial(_nsa_sample_attn_kernel, past=past, n_buf=n_buf, n_pages=n_pages)
    per_seq = lambda rows: pl.BlockSpec((None, rows, HEAD_DIM), lambda b, p, pt: (b, 0, 0))
    grid_spec = pltpu.PrefetchScalarGridSpec(
        num_scalar_prefetch=1,
        grid=(bs, n_pages // PAGES_PER_STEP + 1),
        in_specs=_page_specs(layer, n_pool, n_pages) + [
            pl.BlockSpec((ts, q_cols), lambda b, p, pt: (rb + b, 0)),
            per_seq(page_rows),
            pl.BlockSpec((None, None, N_KV_HEADS, ts, bps), lambda b, p, pt: (b, p, 0, 0, 0)),
            pl.BlockSpec(expand.shape, lambda b, p, pt: (0, 0)),
            per_seq(n_buf * KV_SLOTS),
            per_seq(page_rows),
            pl.BlockSpec((ts, q_cols), lambda b, p, pt: (b, 0)),
            pl.BlockSpec((ts, LANES), lambda b, p, pt: (rb + b, 0))],
        out_specs=pl.BlockSpec((ts, q_cols), lambda b, p, pt: (b, 0)),
        scratch_shapes=[pltpu.VMEM((N_KV_HEADS, GROUP, ts, 1), F32),
                        pltpu.VMEM((N_KV_HEADS, GROUP, ts, 1), F32),
                        pltpu.VMEM((N_KV_HEADS, GROUP, ts, HEAD_DIM), F32)],
    )
    return pl.pallas_call(
        kern,
        grid_spec=grid_spec,
        out_shape=jax.ShapeDtypeStruct((bs * ts, q_cols), F32),
        compiler_params=_params(2),
    )(page_table, *([cache_sel] * PAGES_PER_STEP), q, new_sel, selmask, expand, win_buf, new_win, o_c, gates)


def _gate_weight(w_in, layer):
    wg = w_in[layer:layer + 1, :, N_HEADS * HEAD_DIM + 3 * KV_COLS:]
    return jnp.pad(wg, ((0, 0), (0, 0), (0, LANES - wg.shape[2])))


def _attn_layer(x, layer, shapes, cache_cmp, cache_sel, n_pool, win_buf, page_table,
                norm, w_in, pool_w, pe, w1, w2, w_out):
    n_batch, seq, bs, ts, past = shapes
    m_prompt = n_batch * seq
    q_cols = N_HEADS * HEAD_DIM
    h = _rmsnorm(x, norm, BF16)
    q = _matmul(h, w_in, layer, q_cols, scale=SCALE)
    kv_rows, kv_heads = _matmul_kv(h, w_in, layer, q_cols)
    gates = _matmul(h, _gate_weight(w_in, layer), 0, LANES)

    ha, hb = _halfsums_prompt(kv_rows, pool_w, m_prompt)
    half_rows = seq // CMP_STRIDE * KV_SLOTS
    ckv_p = _cmp_mlp(ha.reshape(n_batch, half_rows, HEAD_DIM), hb.reshape(n_batch, half_rows, HEAD_DIM),
                     pool_w, pe, w1, w2)
    mix_p = _nsa_prompt(q, kv_heads, gates, ckv_p, n_batch, seq)

    sa, sb = _halfsums_paged(cache_cmp, layer, n_pool, page_table, pool_w)
    ckv_s = _cmp_mlp(sa, sb, pool_w, pe, w1, w2)
    o_c, selmask = _nsa_sample_cmp(q, ckv_s, m_prompt, bs, ts, past)
    mix_s = _nsa_sample_attn(q, kv_rows, gates, cache_sel, layer, n_pool, page_table, selmask, win_buf, o_c,
                             m_prompt, bs, ts, past)

    mixed = jnp.concatenate([mix_p, mix_s.astype(BF16)], axis=0)
    x = _matmul_residual(mixed, w_out, layer, x)

    def kv_p(branch):
        return kv_rows[branch, :m_prompt * KV_SLOTS].reshape(n_batch, seq, N_KV_HEADS, 2, HEAD_DIM)

    def kv_s(branch):
        return kv_rows[branch, m_prompt * KV_SLOTS:].reshape(bs, ts, N_KV_HEADS, 2, HEAD_DIM)

    n_buf = win_buf.shape[1] // KV_SLOTS
    win_s = jnp.concatenate([win_buf.reshape(bs, n_buf, N_KV_HEADS, 2, HEAD_DIM), kv_s(2)], axis=1)
    new = (kv_p(0), kv_p(1), kv_p(2)[:, -min(WINDOW, seq):], kv_s(0), kv_s(1), win_s[:, -n_buf:])
    return x, new


def _conv_layer(x, layer, shapes, state, norm, w_in, conv_k, w_out):
    n_batch, seq, bs, ts, _ = shapes
    m_prompt = n_batch * seq
    d = w_out.shape[1]
    h = _rmsnorm(x, norm, BF16)
    bgate, u = _matmul_conv_in(h, w_in, layer)
    k_pad = jnp.pad(conv_k, ((0, SUBLANES - CONV_W), (0, 0)))
    hist_p = jnp.zeros((n_batch, SUBLANES, d), F32)
    hist_s = jnp.pad(state, ((0, 0), (SUBLANES - (CONV_W - 1), 0), (0, 0)))
    z_p = _conv_apply(u, bgate, hist_p, k_pad, 0, n_batch, seq, BF16)
    z_s = _conv_apply(u, bgate, hist_s, k_pad, m_prompt, bs, ts, F32)
    z = jnp.concatenate([z_p, z_s.astype(BF16)], axis=0)
    x = _matmul_residual(z, w_out, layer, x)
    tail = CONV_W - 1
    new_p = jnp.stack([u[(b + 1) * seq - tail:(b + 1) * seq] for b in range(n_batch)])
    new_s = u[m_prompt:].reshape(bs, ts, d)[:, -tail:]
    return x, new_p, new_s


def _ffn(x, layer, norm, w_in, w_out):
    h = _rmsnorm(x, norm, BF16)
    hid = _matmul_swiglu(h, w_in, layer)
    return _matmul_residual(hid, w_out, layer, x)


def kernel(x_prompt, x_sample, cache_cmp_kv, cache_sel_kv, state_win_kv, state_conv, page_table, attn_norm, attn_w_in, attn_cmp_pool, attn_cmp_pe, attn_cmp_w1, attn_cmp_w2, attn_w_out, conv_norm, conv_w_in, conv_kernel, conv_w_out, ffn_norm, ffn_w_in, ffn_w_out, final_norm):
    n_batch, seq, d = x_prompt.shape
    bs, ts, _ = x_sample.shape
    past = page_table.shape[1] * PAGE_SIZE
    n_buf = state_win_kv.shape[2]
    depth = ffn_norm.shape[0]
    assert d == N_HEADS * HEAD_DIM and seq % SEL_SPAN == 0 and seq >= WIN_KEYS
    assert ts == SUBLANES and (n_batch * seq) % ts == 0
    shapes = (n_batch, seq, bs, ts, past)
    m_prompt = n_batch * seq
    n_attn, n_pool = cache_cmp_kv.shape[:2]
    cache_cmp = cache_cmp_kv.reshape(n_attn * n_pool, PAGE_SIZE * KV_SLOTS, HEAD_DIM)
    cache_sel = cache_sel_kv.reshape(n_attn * n_pool, PAGE_SIZE * KV_SLOTS, HEAD_DIM)

    x = jnp.concatenate([x_prompt.reshape(m_prompt, d), x_sample.reshape(bs * ts, d)], axis=0)
    attn_new, conv_new = [], []
    for i in range(depth):
        l = i // 2
        if i % 2 == 0:
            win_buf = state_win_kv[l].reshape(bs, n_buf * KV_SLOTS, HEAD_DIM)
            x, new = _attn_layer(x, l, shapes, cache_cmp, cache_sel, n_pool, win_buf, page_table,
                                 attn_norm[l], attn_w_in, attn_cmp_pool[l], attn_cmp_pe[l],
                                 attn_cmp_w1[l], attn_cmp_w2[l], attn_w_out)
            attn_new.append(new)
        else:
            x, new_p, new_s = _conv_layer(x, l, shapes, state_conv[l], conv_norm[l], conv_w_in,
                                          conv_kernel[l], conv_w_out)
            conv_new.append((new_p, new_s))
        x = _ffn(x, i, ffn_norm[i], ffn_w_in, ffn_w_out)
    y = _rmsnorm(x, final_norm, F32)
    stack = lambda k: jnp.stack([a[k] for a in attn_new])
    return (y[:m_prompt].reshape(n_batch, seq, d), y[m_prompt:].reshape(bs, ts, d),
            stack(0), stack(1), stack(2), jnp.stack([c[0] for c in conv_new]),
            stack(3), stack(4), stack(5), jnp.stack([c[1] for c in conv_new]))
```

```python
import functools

import jax
import jax.numpy as jnp
from jax import lax
from jax.experimental import pallas as pl
from jax.experimental.pallas import tpu as pltpu

N_HEADS = 16
HEAD_DIM = 128
N_KV_HEADS = 4
GROUP = N_HEADS // N_KV_HEADS
KV_SLOTS = N_KV_HEADS * 2
KV_COLS = KV_SLOTS * HEAD_DIM
CMP_BLOCK = 32
CMP_STRIDE = 16
SEL_BLOCK = 64
SEL_SHIFT = 6
TOP_N = 16
N_LOCAL_FORCED = 2
WINDOW = 512
PAGE_SIZE = 128
CONV_W = 3
EPS = 1e-6
SCALE = HEAD_DIM ** -0.5
NEG = -1e30

LANES = 128
SUBLANES = 8
SEL_CHUNK = 512
SEL_SPAN = 512
Q_TILE = 64
PRE_TILE = 256
PAGES_PER_STEP = 8
VMEM_LIMIT = 56 * 1024 * 1024

F32 = jnp.float32
BF16 = jnp.bfloat16


def _params(n_axes):
    return pltpu.CompilerParams(dimension_semantics=("arbitrary",) * n_axes,
                                vmem_limit_bytes=VMEM_LIMIT)


def _pick_tile(n, limit, mult):
    best = None
    for t in range(mult, min(n, limit) + 1, mult):
        if n % t == 0:
            best = t
    assert best is not None, (n, limit, mult)
    return best


def _dot(a, b):
    return jnp.dot(a, b, preferred_element_type=F32)


def _dot_nt(a, b):
    return lax.dot_general(a, b, (((1,), (1,)), ((), ())), preferred_element_type=F32)


def _silu(x):
    return x * jax.nn.sigmoid(x)


def _slot_rows(ref, slot, n_rows):
    return ref[pl.ds(slot, n_rows, stride=KV_SLOTS), :]


def _rmsnorm_kernel(x_ref, g_ref, o_ref):
    x = x_ref[...]
    ms = jnp.mean(x * x, axis=-1, keepdims=True)
    o_ref[...] = ((x * lax.rsqrt(ms + EPS)) * g_ref[...]).astype(o_ref.dtype)


def _rmsnorm(x, gain, out_dtype):
    m, d = x.shape
    tm = _pick_tile(m, 1024, 16)
    return pl.pallas_call(
        _rmsnorm_kernel,
        grid=(m // tm,),
        in_specs=[pl.BlockSpec((tm, d), lambda i: (i, 0)),
                  pl.BlockSpec((1, d), lambda i: (0, 0))],
        out_specs=pl.BlockSpec((tm, d), lambda i: (i, 0)),
        out_shape=jax.ShapeDtypeStruct((m, d), out_dtype),
        compiler_params=_params(1),
    )(x, gain.reshape(1, d))


def _cast_weights_once(w_refs, w_scrs):
    @pl.when(pl.program_id(1) == 0)
    def _():
        for w_ref, w_scr in zip(w_refs, w_scrs):
            w_scr[...] = w_ref[...].astype(BF16)


W_BLOCK_BYTES = 16 * 1024 * 1024


def _w_spec(k, tn, layer, blk0=0):
    single = 2 * 4 * k * tn > W_BLOCK_BYTES
    return pl.BlockSpec((None, k, tn), lambda j, i: (layer, 0, blk0 + j),
                        pipeline_mode=pl.Buffered(1) if single else None)


def _wide_tile(n, k):
    return max(t for t in (LANES, 2 * LANES, 4 * LANES, 8 * LANES) if n % t == 0 and 4 * k * t <= W_BLOCK_BYTES)


def _w_scratch(k, tn, count=1):
    return [pltpu.VMEM((k, tn), BF16) for _ in range(count)]


def _mm_kernel(a_ref, w_ref, o_ref, w_scr, *, scale):
    _cast_weights_once([w_ref], [w_scr])
    acc = _dot(a_ref[...], w_scr[...])
    o_ref[...] = acc if scale is None else acc * scale


def _matmul(a, w, layer, n, scale=None):
    m, k = a.shape
    tm = _pick_tile(m, 1024, 16)
    tn = _wide_tile(n, k)
    return pl.pallas_call(
        functools.partial(_mm_kernel, scale=scale),
        grid=(n // tn, m // tm),
        in_specs=[pl.BlockSpec((tm, k), lambda j, i: (i, 0)), _w_spec(k, tn, layer)],
        out_specs=pl.BlockSpec((tm, tn), lambda j, i: (i, j)),
        out_shape=jax.ShapeDtypeStruct((m, n), F32),
        scratch_shapes=_w_scratch(k, tn),
        compiler_params=_params(2),
    )(a, w)


def _mm_kv_kernel(a_ref, w_ref, rows_ref, heads_ref, w_scr):
    _cast_weights_once([w_ref], [w_scr])
    acc = _dot(a_ref[...], w_scr[...])
    tm = acc.shape[0]
    for c in range(KV_SLOTS):
        part = acc[:, c * HEAD_DIM:(c + 1) * HEAD_DIM]
        rows_ref[pl.ds(c, tm, stride=KV_SLOTS), :] = part
        heads_ref[c] = part.astype(heads_ref.dtype)


def _matmul_kv(a, w, layer, col0):
    m, k = a.shape
    tm = _pick_tile(m, 1024, 16)
    return pl.pallas_call(
        _mm_kv_kernel,
        grid=(3, m // tm),
        in_specs=[pl.BlockSpec((tm, k), lambda j, i: (i, 0)), _w_spec(k, KV_COLS, layer, col0 // KV_COLS)],
        out_specs=[pl.BlockSpec((None, tm * KV_SLOTS, HEAD_DIM), lambda j, i: (j, i, 0)),
                   pl.BlockSpec((None, KV_SLOTS, tm, HEAD_DIM), lambda j, i: (j, 0, i, 0))],
        out_shape=[jax.ShapeDtypeStruct((3, m * KV_SLOTS, HEAD_DIM), F32),
                   jax.ShapeDtypeStruct((3, KV_SLOTS, m, HEAD_DIM), BF16)],
        scratch_shapes=_w_scratch(k, KV_COLS),
        compiler_params=_params(2),
    )(a, w)


def _mm_res_kernel(a_ref, w_ref, r_ref, o_ref, w_scr):
    _cast_weights_once([w_ref], [w_scr])
    o_ref[...] = r_ref[...] + _dot(a_ref[...], w_scr[...])


def _matmul_residual(a, w, layer, res):
    m, k = a.shape
    n = w.shape[2]
    tm = _pick_tile(m, 1024, 16)
    tn = _wide_tile(n, k)
    return pl.pallas_call(
        _mm_res_kernel,
        grid=(n // tn, m // tm),
        in_specs=[pl.BlockSpec((tm, k), lambda j, i: (i, 0)), _w_spec(k, tn, layer),
                  pl.BlockSpec((tm, tn), lambda j, i: (i, j))],
        out_specs=pl.BlockSpec((tm, tn), lambda j, i: (i, j)),
        out_shape=jax.ShapeDtypeStruct((m, n), F32),
        scratch_shapes=_w_scratch(k, tn),
        compiler_params=_params(2),
    )(a, w, res)


def _mm_swiglu_kernel(a_ref, wg_ref, wu_ref, o_ref, wg_scr, wu_scr):
    _cast_weights_once([wg_ref, wu_ref], [wg_scr, wu_scr])
    a = a_ref[...]
    gate = _dot(a, wg_scr[...])
    up = _dot(a, wu_scr[...])
    o_ref[...] = (_silu(gate) * up).astype(o_ref.dtype)


def _matmul_swiglu(a, w, layer):
    m, k = a.shape
    f = w.shape[2] // 2
    tm = _pick_tile(m, 1024, 16)
    tn = _pick_tile(f, 512, LANES)
    nb = f // tn
    return pl.pallas_call(
        _mm_swiglu_kernel,
        grid=(nb, m // tm),
        in_specs=[pl.BlockSpec((tm, k), lambda j, i: (i, 0)), _w_spec(k, tn, layer), _w_spec(k, tn, layer, nb)],
        out_specs=pl.BlockSpec((tm, tn), lambda j, i: (i, j)),
        out_shape=jax.ShapeDtypeStruct((m, f), BF16),
        scratch_shapes=_w_scratch(k, tn, 2),
        compiler_params=_params(2),
    )(a, w, w)


def _mm_conv_in_kernel(a_ref, wb_ref, wc_ref, wx_ref, b_ref, u_ref, wb_scr, wc_scr, wx_scr):
    _cast_weights_once([wb_ref, wc_ref, wx_ref], [wb_scr, wc_scr, wx_scr])
    a = a_ref[...]
    b_ref[...] = _dot(a, wb_scr[...])
    u_ref[...] = _dot(a, wc_scr[...]) * _dot(a, wx_scr[...])


def _matmul_conv_in(a, w, layer):
    m, k = a.shape
    dc = w.shape[2] // 3
    tm = _pick_tile(m, 1024, 16)
    tn = _pick_tile(dc, 512, LANES)
    nb = dc // tn
    out = jax.ShapeDtypeStruct((m, dc), F32)
    return pl.pallas_call(
        _mm_conv_in_kernel,
        grid=(nb, m // tm),
        in_specs=[pl.BlockSpec((tm, k), lambda j, i: (i, 0)),
                  _w_spec(k, tn, layer), _w_spec(k, tn, layer, nb), _w_spec(k, tn, layer, 2 * nb)],
        out_specs=[pl.BlockSpec((tm, tn), lambda j, i: (i, j)),
                   pl.BlockSpec((tm, tn), lambda j, i: (i, j))],
        out_shape=[out, out],
        scratch_shapes=_w_scratch(k, tn, 3),
        compiler_params=_params(2),
    )(a, w, w, w)


def _conv_apply_kernel(u_ref, b_ref, prev_ref, hist_ref, k_ref, o_ref):
    i = pl.program_id(1)
    u = u_ref[...]
    rows = u.shape[0]
    tail = jnp.where(i == 0, hist_ref[...], prev_ref[...])
    row = lax.broadcasted_iota(jnp.int32, u.shape, 0)
    u1 = jnp.where(row == 0, tail[7:8], pltpu.roll(u, 1, axis=0))
    u2 = jnp.where(row == 0, tail[6:7], jnp.where(row == 1, tail[7:8], pltpu.roll(u, 2 % rows, axis=0)))
    k = k_ref[...]
    v = k[0:1] * u2 + k[1:2] * u1 + k[2:3] * u
    o_ref[...] = (b_ref[...] * v).astype(o_ref.dtype)


def _conv_apply(u, bgate, hist, conv_k, row0, n_seq, seq_len, out_dtype):
    d = u.shape[1]
    tt = _pick_tile(seq_len, 512, SUBLANES)
    tc = _pick_tile(d, 1024, LANES)
    tiles = seq_len // tt
    blk0 = row0 // tt
    sub = tt // SUBLANES

    def cur(s, i, j):
        return (blk0 + s * tiles + i, j)

    def prev(s, i, j):
        return (jnp.maximum((blk0 + s * tiles + i) * sub - 1, 0), j)

    return pl.pallas_call(
        _conv_apply_kernel,
        grid=(n_seq, tiles, d // tc),
        in_specs=[pl.BlockSpec((tt, tc), cur),
                  pl.BlockSpec((tt, tc), cur),
                  pl.BlockSpec((SUBLANES, tc), prev),
                  pl.BlockSpec((None, SUBLANES, tc), lambda s, i, j: (s, 0, j)),
                  pl.BlockSpec((SUBLANES, tc), lambda s, i, j: (0, j))],
        out_specs=pl.BlockSpec((tt, tc), lambda s, i, j: (s * tiles + i, j)),
        out_shape=jax.ShapeDtypeStruct((n_seq * seq_len, d), out_dtype),
        compiler_params=_params(3),
    )(u, bgate, u, hist, conv_k)


def _chunk_halfsums(x, wa, wb):
    n = x.shape[0] // (CMP_STRIDE * KV_SLOTS)
    x4 = x.reshape(n, CMP_STRIDE, KV_SLOTS, HEAD_DIM)
    a = jnp.sum(x4 * wa[None], axis=1).reshape(n * KV_SLOTS, HEAD_DIM)
    b = jnp.sum(x4 * wb[None], axis=1).reshape(n * KV_SLOTS, HEAD_DIM)
    return a, b


def _halfsum_kernel(x_ref, wa_ref, wb_ref, a_ref, b_ref):
    a_ref[...], b_ref[...] = _chunk_halfsums(x_ref[...], wa_ref[...], wb_ref[...])


def _halfsum_paged_kernel(pt_ref, *refs):
    del pt_ref
    page_refs = refs[:PAGES_PER_STEP]
    wa_ref, wb_ref, a_ref, b_ref = refs[PAGES_PER_STEP:]
    rows = (PAGE_SIZE // CMP_STRIDE) * KV_SLOTS
    for k, page_ref in enumerate(page_refs):
        a, b = _chunk_halfsums(page_ref[...], wa_ref[...], wb_ref[...])
        a_ref[k * rows:(k + 1) * rows, :] = a
        b_ref[k * rows:(k + 1) * rows, :] = b


def _pool_weight_maps(pool_w):
    w = pool_w.reshape(2, CMP_BLOCK // CMP_STRIDE, CMP_STRIDE)
    maps = []
    for half in range(2):
        wj = jnp.tile(w[:, half, :].T, (1, N_KV_HEADS))
        maps.append(jnp.broadcast_to(wj[:, :, None], (CMP_STRIDE, KV_SLOTS, HEAD_DIM)))
    return maps


def _halfsums_prompt(kv_rows, pool_w, n_rows):
    wa, wb = _pool_weight_maps(pool_w)
    tr = _pick_tile(n_rows, 512, LANES)
    out_rows = tr // CMP_STRIDE * KV_SLOTS
    out = jax.ShapeDtypeStruct((n_rows // CMP_STRIDE * KV_SLOTS, HEAD_DIM), F32)
    wspec = pl.BlockSpec((CMP_STRIDE, KV_SLOTS, HEAD_DIM), lambda i: (0, 0, 0))
    ospec = pl.BlockSpec((out_rows, HEAD_DIM), lambda i: (i, 0))
    return pl.pallas_call(
        _halfsum_kernel,
        grid=(n_rows // tr,),
        in_specs=[pl.BlockSpec((None, tr * KV_SLOTS, HEAD_DIM), lambda i: (0, i, 0)), wspec, wspec],
        out_specs=[ospec, ospec],
        out_shape=[out, out],
        compiler_params=_params(1),
    )(kv_rows, wa, wb)


def _page_specs(layer, n_pool, n_pages):
    def spec(k):
        def index(b, p, pt):
            page = jnp.minimum(p * PAGES_PER_STEP + k, n_pages - 1)
            return (layer * n_pool + pt[b, page], 0, 0)
        return pl.BlockSpec((None, PAGE_SIZE * KV_SLOTS, HEAD_DIM), index)
    return [spec(k) for k in range(PAGES_PER_STEP)]


def _halfsums_paged(cache, layer, n_pool, page_table, pool_w):
    bs, n_pages = page_table.shape
    assert n_pages % PAGES_PER_STEP == 0
    wa, wb = _pool_weight_maps(pool_w)
    rows = PAGES_PER_STEP * (PAGE_SIZE // CMP_STRIDE) * KV_SLOTS
    out = jax.ShapeDtypeStruct((bs, n_pages * (PAGE_SIZE // CMP_STRIDE) * KV_SLOTS, HEAD_DIM), F32)
    wspec = pl.BlockSpec((CMP_STRIDE, KV_SLOTS, HEAD_DIM), lambda b, p, pt: (0, 0, 0))
    ospec = pl.BlockSpec((None, rows, HEAD_DIM), lambda b, p, pt: (b, p, 0))
    grid_spec = pltpu.PrefetchScalarGridSpec(
        num_scalar_prefetch=1,
        grid=(bs, n_pages // PAGES_PER_STEP),
        in_specs=_page_specs(layer, n_pool, n_pages) + [wspec, wspec],
        out_specs=[ospec, ospec],
    )
    return pl.pallas_call(
        _halfsum_paged_kernel,
        grid_spec=grid_spec,
        out_shape=[out, out],
        compiler_params=_params(2),
    )(page_table, *([cache] * PAGES_PER_STEP), wa, wb)


def _cmp_mlp_kernel(a_ref, b_ref, pw_ref, pe_ref, w1_ref, w2_ref, o_ref):
    n = o_ref.shape[1]
    for k in range(2):
        pe_term = jnp.sum(pw_ref[k] * pe_ref[k], axis=0, keepdims=True)
        w1 = w1_ref[k].astype(BF16)
        w2 = w2_ref[k].astype(BF16)
        for g in range(N_KV_HEADS):
            c = 2 * g + k
            x = _slot_rows(a_ref, c, n) + pltpu.roll(_slot_rows(b_ref, c, n), n - 1, axis=0) + pe_term
            hid = _silu(_dot(x.astype(BF16), w1))
            o_ref[c] = _dot(hid.astype(BF16), w2)


def _cmp_mlp(half_a, half_b, pool_w, pe, w1, w2):
    n_seq, rows, _ = half_a.shape
    n_chunks = rows // KV_SLOTS
    spec = pl.BlockSpec((None, rows, HEAD_DIM), lambda b: (b, 0, 0))
    full3 = lambda shape: pl.BlockSpec(shape, lambda b: (0, 0, 0))
    return pl.pallas_call(
        _cmp_mlp_kernel,
        grid=(n_seq,),
        in_specs=[spec, spec, full3((2, CMP_BLOCK, 1)), full3((2, CMP_BLOCK, HEAD_DIM)),
                  full3((2, HEAD_DIM, HEAD_DIM)), full3((2, HEAD_DIM, HEAD_DIM))],
        out_specs=pl.BlockSpec((None, KV_SLOTS, n_chunks, HEAD_DIM), lambda b: (b, 0, 0, 0)),
        out_shape=jax.ShapeDtypeStruct((n_seq, KV_SLOTS, n_chunks, HEAD_DIM), F32),
        compiler_params=_params(1),
    )(half_a, half_b, pool_w.reshape(2, CMP_BLOCK, 1), pe, w1, w2)


def _overlap_matrix(n_cmp_pad, n_sel_pad):
    c_start = jnp.arange(n_cmp_pad)[:, None] * CMP_STRIDE
    b_start = jnp.arange(n_sel_pad)[None, :] * SEL_BLOCK
    ov = (c_start < b_start + SEL_BLOCK) & (c_start + CMP_BLOCK > b_start)
    return ov.astype(BF16)


def _stack_heads(q):
    return jnp.concatenate([q[:, r * HEAD_DIM:(r + 1) * HEAD_DIM] for r in range(GROUP)], axis=0)


def _cmp_attend(q4, kc, vc, tpos):
    t = tpos.shape[0]
    n_cmp = kc.shape[0]
    s = _dot_nt(q4, kc).reshape(GROUP, t, n_cmp)
    last = lax.broadcasted_iota(jnp.int32, (1, n_cmp), 1) * CMP_STRIDE + (CMP_BLOCK - 1)
    ok = (last <= tpos)[None]
    sm = jnp.where(ok, s, NEG)
    m = jnp.max(sm, axis=-1, keepdims=True)
    e = jnp.where(ok, jnp.exp(sm - m), 0.0)
    p = e / jnp.maximum(jnp.sum(e, axis=-1, keepdims=True), 1e-30)
    pb = p.astype(BF16).reshape(GROUP * t, n_cmp)
    o_c = _dot(pb, vc).reshape(GROUP, t, HEAD_DIM)
    return o_c, pb


def _block_scores(imp, blk, tpos, n_sel):
    cur = tpos >> SEL_SHIFT
    real = blk < n_sel
    valid = (blk * SEL_BLOCK <= tpos) & real
    forced = ((blk == 0) | ((blk <= cur) & (blk > cur - N_LOCAL_FORCED))) & real
    return jnp.where(forced, jnp.inf, jnp.where(valid, imp, -jnp.inf))


def _masked_online_step(q4, k, v, bias, carry):
    m_i, l_i, acc = carry
    t, n_keys = bias.shape
    s = _dot_nt(q4, k).reshape(GROUP, t, n_keys) + bias[None]
    m_new = jnp.maximum(m_i, jnp.max(s, axis=-1, keepdims=True))
    alpha = jnp.exp(m_i - m_new)
    p = jnp.exp(s - m_new)
    l_new = alpha * l_i + jnp.sum(p, axis=-1, keepdims=True)
    pv = _dot(p.astype(BF16).reshape(GROUP * t, n_keys), v).reshape(GROUP, t, HEAD_DIM)
    return m_new, l_new, alpha * acc + pv


def _online_init(t):
    return (jnp.full((GROUP, t, 1), NEG, F32), jnp.zeros((GROUP, t, 1), F32),
            jnp.zeros((GROUP, t, HEAD_DIM), F32))


def _fold_lane_tiles(x, op):
    out = x[..., 0:LANES]
    for j in range(1, x.shape[-1] // LANES):
        out = op(out, x[..., j * LANES:(j + 1) * LANES])
    return out


def _gated_mix(gate, lane0, o_c, o_s, o_w):
    outs = []
    for r in range(GROUP):
        lanes = [lane0 + branch * N_HEADS + r for branch in range(3)]
        outs.append(gate[:, lanes[0]:lanes[0] + 1] * o_c[r]
                    + gate[:, lanes[1]:lanes[1] + 1] * o_s[r]
                    + gate[:, lanes[2]:lanes[2] + 1] * o_w[r])
    return outs


def _topk_mask_by_rank(score_t, k, n_live):
    n, lanes = score_t.shape
    tiles = [score_t[v * SUBLANES:(v + 1) * SUBLANES] for v in range(n_live // SUBLANES)]
    ranks = [jnp.zeros((SUBLANES, lanes), F32) for _ in tiles]
    row = lax.broadcasted_iota(jnp.int32, (SUBLANES, 1), 0)
    for i in range(n_live):
        si = score_t[i:i + 1, :]
        for v, tile in enumerate(tiles):
            lo = v * SUBLANES
            if lo > i:
                ahead = jnp.where(si >= tile, 1.0, 0.0)
            elif lo + SUBLANES - 1 < i:
                ahead = jnp.where(si > tile, 1.0, 0.0)
            else:
                tie = jnp.where(row + lo > i, 1.0, 0.0)
                ahead = jnp.where(si > tile, 1.0, jnp.where(si == tile, tie, 0.0))
            ranks[v] = ranks[v] + ahead
    live = [jnp.where(tile > -jnp.inf, jnp.where(rank < k, 1.0, 0.0), 0.0) for tile, rank in zip(tiles, ranks)]
    dead = [jnp.zeros((n - n_live, lanes), F32)] if n > n_live else []
    return jnp.concatenate(live + dead, axis=0)


def _head_gate_lanes(gate_ref):
    return pltpu.roll(jax.nn.sigmoid(gate_ref[...]), LANES - GROUP * pl.program_id(1), axis=1)


def _nsa_prompt_pre_kernel(q_ref, kw_ref, vw_ref, kc_ref, vc_ref, gate_ref, ov_ref, part_ref, score_ref):
    c = pl.program_id(2)
    tq = q_ref.shape[0]
    n_sel = ov_ref.shape[0]
    start = c * tq
    q4 = _stack_heads(q_ref[...]).astype(BF16)
    tpos = start + lax.broadcasted_iota(jnp.int32, (tq, 1), 0)

    o_c, pb = _cmp_attend(q4, kc_ref[...].astype(BF16), vc_ref[...].astype(BF16), tpos)

    imp_all = _dot_nt(ov_ref[...], pb)
    imp_t = imp_all[:, 0:tq]
    for r in range(1, GROUP):
        imp_t = imp_t + imp_all[:, r * tq:(r + 1) * tq]
    tpos_t = start + lax.broadcasted_iota(jnp.int32, (1, tq), 1)
    blk_t = lax.broadcasted_iota(jnp.int32, (n_sel, 1), 0)
    score_ref[...] = _block_scores(imp_t, blk_t, tpos_t, n_sel)

    win_keys = WINDOW + tq
    lo = pl.multiple_of(jnp.maximum(start - WINDOW, 0), LANES)
    rel = tpos - (lo + lax.broadcasted_iota(jnp.int32, (1, win_keys), 1))
    bias_w = jnp.where((rel >= 0) & (rel < WINDOW), 0.0, NEG)
    _, l_w, acc_w = _masked_online_step(q4, kw_ref[pl.ds(lo, win_keys), :], vw_ref[pl.ds(lo, win_keys), :],
                                        bias_w, _online_init(tq))
    o_w = acc_w / l_w

    gate = _head_gate_lanes(gate_ref)
    part_ref[...] = jnp.concatenate(
        [gate[:, r:r + 1] * o_c[r] + gate[:, 2 * N_HEADS + r:2 * N_HEADS + r + 1] * o_w[r] for r in range(GROUP)],
        axis=1)


def _nsa_prompt_sel_kernel(q_ref, ks_ref, vs_ref, score_ref, gate_ref, part_ref, exp_ref, o_ref,
                           sel_scr, os_scr, s_scr):
    c = pl.program_id(2)
    tq = Q_TILE
    seq = ks_ref.shape[0]
    n_sel = score_ref.shape[0]
    start = c * tq
    q4 = _stack_heads(q_ref[...]).astype(BF16)
    tpos = start + lax.broadcasted_iota(jnp.int32, (tq, 1), 0)
    qb_per_span = SEL_SPAN // tq

    def in_span(v):
        return (c >= v * qb_per_span) & (c < (v + 1) * qb_per_span)

    for v in range(seq // SEL_SPAN):
        @pl.when(in_span(v) & ((c & 1) == 0))
        def _():
            sel_t = _topk_mask_by_rank(score_ref[...], min(TOP_N, n_sel), (v + 1) * SEL_SPAN // SEL_BLOCK)
            sel_sq = jnp.concatenate([sel_t, jnp.zeros((LANES - n_sel, LANES), F32)], axis=0)
            sel_scr[...] = sel_sq.T

    selb = sel_scr[pl.ds(pl.multiple_of((c & 1) * tq, tq), tq), :].astype(BF16)

    for v in range(seq // SEL_SPAN):
        @pl.when(in_span(v))
        def _():
            n_keys = (v + 1) * SEL_SPAN
            m_lane = jnp.full((GROUP, tq, LANES), NEG, F32)
            for k in range(n_keys // SEL_CHUNK):
                off = k * SEL_CHUNK
                picked = _dot(selb, exp_ref[k])
                kpos = off + lax.broadcasted_iota(jnp.int32, (1, SEL_CHUNK), 1)
                bias = jnp.where(picked > 0.5, jnp.where(kpos <= tpos, 0.0, NEG), NEG)
                s_k = _dot_nt(q4, ks_ref[off:off + SEL_CHUNK, :]).reshape(GROUP, tq, SEL_CHUNK) + bias[None]
                s_scr[k] = s_k
                m_lane = jnp.maximum(m_lane, _fold_lane_tiles(s_k, jnp.maximum))
            m_s = jnp.max(m_lane, axis=-1, keepdims=True)
            l_lane = jnp.zeros((GROUP, tq, LANES), F32)
            acc = jnp.zeros((GROUP * tq, HEAD_DIM), F32)
            for k in range(n_keys // SEL_CHUNK):
                off = k * SEL_CHUNK
                p_k = jnp.exp(s_scr[k] - m_s)
                l_lane = l_lane + _fold_lane_tiles(p_k, jnp.add)
                acc = acc + _dot(p_k.astype(BF16).reshape(GROUP * tq, SEL_CHUNK), vs_ref[off:off + SEL_CHUNK, :])
            os_scr[...] = acc.reshape(GROUP, tq, HEAD_DIM) / jnp.sum(l_lane, axis=-1, keepdims=True)

    gate = _head_gate_lanes(gate_ref)
    o_s = os_scr[...]
    mix = jnp.concatenate([gate[:, N_HEADS + r:N_HEADS + r + 1] * o_s[r] for r in range(GROUP)], axis=1)
    o_ref[...] = (part_ref[...] + mix).astype(o_ref.dtype)


def _nsa_prompt(q, kv_heads, gates, ckv, n_batch, seq):
    n_qb = seq // Q_TILE
    n_pre = seq // PRE_TILE
    n_cmp = seq // CMP_STRIDE
    n_sel = seq // SEL_BLOCK
    n_chunks = seq // SEL_CHUNK
    qw = GROUP * HEAD_DIM
    assert 2 * Q_TILE == LANES and n_sel <= LANES and n_sel % SUBLANES == 0
    assert seq % PRE_TILE == 0 and PRE_TILE % LANES == 0 and seq >= WINDOW + PRE_TILE
    overlap_t = _overlap_matrix(n_cmp, n_sel).T
    key_blk = jnp.arange(seq) // SEL_BLOCK
    expand = (jnp.arange(LANES)[:, None] == key_blk[None, :]).astype(BF16)
    expand = expand.reshape(LANES, n_chunks, SEL_CHUNK).transpose(1, 0, 2)

    def kv_spec(branch, kv):
        return pl.BlockSpec((None, None, seq, HEAD_DIM), lambda b, g, c: (branch, 2 * g + kv, b, 0))

    def ckv_spec(kv):
        return pl.BlockSpec((None, None, n_cmp, HEAD_DIM), lambda b, g, c: (b, 2 * g + kv, 0, 0))

    part, score = pl.pallas_call(
        _nsa_prompt_pre_kernel,
        grid=(n_batch, N_KV_HEADS, n_pre),
        in_specs=[pl.BlockSpec((PRE_TILE, qw), lambda b, g, c: (b * n_pre + c, g)),
                  kv_spec(2, 0), kv_spec(2, 1), ckv_spec(0), ckv_spec(1),
                  pl.BlockSpec((PRE_TILE, LANES), lambda b, g, c: (b * n_pre + c, 0)),
                  pl.BlockSpec((n_sel, n_cmp), lambda b, g, c: (0, 0))],
        out_specs=[pl.BlockSpec((PRE_TILE, qw), lambda b, g, c: (b * n_pre + c, g)),
                   pl.BlockSpec((None, None, n_sel, PRE_TILE), lambda b, g, c: (b, g, 0, c))],
        out_shape=[jax.ShapeDtypeStruct((n_batch * seq, N_HEADS * HEAD_DIM), F32),
                   jax.ShapeDtypeStruct((n_batch, N_KV_HEADS, n_sel, seq), F32)],
        compiler_params=_params(3),
    )(q, kv_heads, kv_heads, ckv, ckv, gates, overlap_t)

    return pl.pallas_call(
        _nsa_prompt_sel_kernel,
        grid=(n_batch, N_KV_HEADS, n_qb),
        in_specs=[pl.BlockSpec((Q_TILE, qw), lambda b, g, c: (b * n_qb + c, g)),
                  kv_spec(1, 0), kv_spec(1, 1),
                  pl.BlockSpec((None, None, n_sel, LANES), lambda b, g, c: (b, g, 0, c // 2)),
                  pl.BlockSpec((Q_TILE, LANES), lambda b, g, c: (b * n_qb + c, 0)),
                  pl.BlockSpec((Q_TILE, qw), lambda b, g, c: (b * n_qb + c, g)),
                  pl.BlockSpec((n_chunks, LANES, SEL_CHUNK), lambda b, g, c: (0, 0, 0))],
        out_specs=pl.BlockSpec((Q_TILE, qw), lambda b, g, c: (b * n_qb + c, g)),
        out_shape=jax.ShapeDtypeStruct((n_batch * seq, N_HEADS * HEAD_DIM), BF16),
        scratch_shapes=[pltpu.VMEM((LANES, LANES), F32),
                        pltpu.VMEM((GROUP, Q_TILE, HEAD_DIM), F32),
                        pltpu.VMEM((n_chunks, GROUP, Q_TILE, SEL_CHUNK), F32)],
        compiler_params=_params(3),
    )(q, kv_heads, kv_heads, score, gates, part, expand)


def _topk_mask_by_extraction(score, k):
    t, n = score.shape
    col = lax.broadcasted_iota(jnp.int32, (t, n), 1).astype(F32)
    sel = jnp.zeros((t, n), F32)
    for _ in range(k):
        m = jnp.max(score, axis=-1, keepdims=True)
        first = jnp.min(jnp.where(score == m, col, float(n)), axis=-1, keepdims=True)
        pick = (col == first) & (m > -jnp.inf)
        sel = jnp.where(pick, 1.0, sel)
        score = jnp.where(col == first, -jnp.inf, score)
    return sel


def _nsa_sample_cmp_kernel(q_ref, ckv_ref, ov_ref, oc_ref, sel_ref, *, past, n_sel):
    ts = q_ref.shape[0]
    n_sel_pad = ov_ref.shape[1]
    tpos = past + lax.broadcasted_iota(jnp.int32, (ts, 1), 0)
    blk = lax.broadcasted_iota(jnp.int32, (ts, n_sel_pad), 1)
    outs, scores = [], []
    for g in range(N_KV_HEADS):
        q4 = _stack_heads(q_ref[:, g * GROUP * HEAD_DIM:(g + 1) * GROUP * HEAD_DIM]).astype(BF16)
        o_c, pb = _cmp_attend(q4, ckv_ref[2 * g].astype(BF16), ckv_ref[2 * g + 1].astype(BF16), tpos)
        outs += [o_c[r] for r in range(GROUP)]
        imp = jnp.sum(_dot(pb, ov_ref[...]).reshape(GROUP, ts, n_sel_pad), axis=0)
        scores.append(_block_scores(imp, blk, tpos, n_sel))
    oc_ref[...] = jnp.concatenate(outs, axis=1)
    sel = _topk_mask_by_extraction(jnp.concatenate(scores, axis=0), min(TOP_N, n_sel))
    for g in range(N_KV_HEADS):
        sel_ref[g] = sel[g * ts:(g + 1) * ts]


def _nsa_sample_cmp(q, ckv, row0, bs, ts, past):
    n_cmp = ckv.shape[2]
    n_sel = -(-(past + ts) // SEL_BLOCK)
    n_sel_pad = -(-n_sel // LANES) * LANES
    q_cols = N_HEADS * HEAD_DIM
    overlap = _overlap_matrix(n_cmp, n_sel_pad)
    kern = functools.partial(_nsa_sample_cmp_kernel, past=past, n_sel=n_sel)
    return pl.pallas_call(
        kern,
        grid=(bs,),
        in_specs=[pl.BlockSpec((ts, q_cols), lambda b: (row0 // ts + b, 0)),
                  pl.BlockSpec((None, KV_SLOTS, n_cmp, HEAD_DIM), lambda b: (b, 0, 0, 0)),
                  pl.BlockSpec((n_cmp, n_sel_pad), lambda b: (0, 0))],
        out_specs=[pl.BlockSpec((ts, q_cols), lambda b: (b, 0)),
                   pl.BlockSpec((None, N_KV_HEADS, ts, n_sel_pad), lambda b: (b, 0, 0, 0))],
        out_shape=[jax.ShapeDtypeStruct((bs * ts, q_cols), F32),
                   jax.ShapeDtypeStruct((bs, N_KV_HEADS, ts, n_sel_pad), F32)],
        compiler_params=_params(1),
    )(q, ckv, overlap)


def _nsa_sample_attn_kernel(pt_ref, *refs, past, n_buf, n_pages):
    del pt_ref
    page_refs = refs[:PAGES_PER_STEP]
    (q_ref, newsel_ref, selmask_ref, exp_ref, winbuf_ref, newwin_ref, oc_ref, gate_ref,
     o_ref, m_scr, l_scr, acc_scr) = refs[PAGES_PER_STEP:]
    step = pl.program_id(1)
    n_steps = n_pages // PAGES_PER_STEP
    ts = q_ref.shape[0]
    tpos = past + lax.broadcasted_iota(jnp.int32, (ts, 1), 0)
    q = q_ref[...]

    def q_heads(g):
        return _stack_heads(q[:, g * GROUP * HEAD_DIM:(g + 1) * GROUP * HEAD_DIM]).astype(BF16)

    @pl.when(step == 0)
    def _():
        m_scr[...] = jnp.full(m_scr.shape, NEG, F32)
        l_scr[...] = jnp.zeros(l_scr.shape, F32)
        acc_scr[...] = jnp.zeros(acc_scr.shape, F32)

    def sel_update(kv_refs, first_page):
        n_keys = len(kv_refs) * PAGE_SIZE
        expand = exp_ref[:, 0:n_keys]
        kpos = first_page * PAGE_SIZE + lax.broadcasted_iota(jnp.int32, (1, n_keys), 1)
        for g in range(N_KV_HEADS):
            k = jnp.concatenate([_slot_rows(r, 2 * g, PAGE_SIZE) for r in kv_refs], axis=0).astype(BF16)
            v = jnp.concatenate([_slot_rows(r, 2 * g + 1, PAGE_SIZE) for r in kv_refs], axis=0).astype(BF16)
            picked = _dot(selmask_ref[g].astype(BF16), expand)
            bias = jnp.where((picked > 0.5) & (kpos <= tpos), 0.0, NEG)
            m_scr[g], l_scr[g], acc_scr[g] = _masked_online_step(
                q_heads(g), k, v, bias, (m_scr[g], l_scr[g], acc_scr[g]))

    @pl.when(step < n_steps)
    def _():
        sel_update(page_refs, step * PAGES_PER_STEP)

    @pl.when(step == n_steps)
    def _():
        sel_update([newsel_ref], n_pages)
        gate = jax.nn.sigmoid(gate_ref[...])
        rel_buf = tpos - (past - n_buf + lax.broadcasted_iota(jnp.int32, (1, n_buf), 1))
        bias_buf = jnp.where((rel_buf >= 0) & (rel_buf < WINDOW), 0.0, NEG)
        rel_new = tpos - (past + lax.broadcasted_iota(jnp.int32, (1, PAGE_SIZE), 1))
        bias_new = jnp.where((rel_new >= 0) & (rel_new < WINDOW), 0.0, NEG)
        outs = []
        for g in range(N_KV_HEADS):
            qg = q_heads(g)
            o_s = acc_scr[g] / l_scr[g]
            carry = _masked_online_step(qg, _slot_rows(winbuf_ref, 2 * g, n_buf).astype(BF16),
                                        _slot_rows(winbuf_ref, 2 * g + 1, n_buf).astype(BF16),
                                        bias_buf, _online_init(ts))
            _, l_w, acc_w = _masked_online_step(qg, _slot_rows(newwin_ref, 2 * g, PAGE_SIZE).astype(BF16),
                                                _slot_rows(newwin_ref, 2 * g + 1, PAGE_SIZE).astype(BF16),
                                                bias_new, carry)
            o_w = acc_w / l_w
            h0 = g * GROUP * HEAD_DIM
            o_c = [oc_ref[:, h0 + r * HEAD_DIM:h0 + (r + 1) * HEAD_DIM] for r in range(GROUP)]
            outs += _gated_mix(gate, g * GROUP, o_c, o_s, o_w)
        o_ref[...] = jnp.concatenate(outs, axis=1)


def _nsa_sample_attn(q, kv_rows, gates, cache_sel, layer, n_pool, page_table, selmask, win_buf, o_c,
                     row0, bs, ts, past):
    n_pages = page_table.shape[1]
    assert n_pages % PAGES_PER_STEP == 0
    n_buf = win_buf.shape[1] // KV_SLOTS
    n_sel_pad = selmask.shape[3]
    q_cols = N_HEADS * HEAD_DIM
    rb = row0 // ts
    page_rows = PAGE_SIZE * KV_SLOTS
    bps = PAGES_PER_STEP * PAGE_SIZE // SEL_BLOCK
    assert n_sel_pad % bps == 0
    selmask = selmask.reshape(bs, N_KV_HEADS, ts, n_sel_pad // bps, bps).transpose(0, 3, 1, 2, 4)
    key_blk = jnp.arange(PAGES_PER_STEP * PAGE_SIZE) // SEL_BLOCK
    expand = (jnp.arange(bps)[:, None] == key_blk[None, :]).astype(BF16)
    pad = ((0, 0), (0, (PAGE_SIZE - ts) * KV_SLOTS), (0, 0))
    new_sel = jnp.pad(kv_rows[1, row0 * KV_SLOTS:].reshape(bs, ts * KV_SLOTS, HEAD_DIM), pad)
    new_win = jnp.pad(kv_rows[2, row0 * KV_SLOTS:].reshape(bs, ts * KV_SLOTS, HEAD_DIM), pad)
    kern = functools.partial(_nsa_sample_attn_kernel, past=past, n_buf=n_buf, n_pages=n_pages)
    per_seq = lambda rows: pl.BlockSpec((None, rows, HEAD_DIM), lambda b, p, pt: (b, 0, 0))
    grid_spec = pltpu.PrefetchScalarGridSpec(
        num_scalar_prefetch=1,
        grid=(bs, n_pages // PAGES_PER_STEP + 1),
        in_specs=_page_specs(layer, n_pool, n_pages) + [
            pl.BlockSpec((ts, q_cols), lambda b, p, pt: (rb + b, 0)),
            per_seq(page_rows),
            pl.BlockSpec((None, None, N_KV_HEADS, ts, bps), lambda b, p, pt: (b, p, 0, 0, 0)),
            pl.BlockSpec(expand.shape, lambda b, p, pt: (0, 0)),
            per_seq(n_buf * KV_SLOTS),
            per_seq(page_rows),
            pl.BlockSpec((ts, q_cols), lambda b, p, pt: (b, 0)),
            pl.BlockSpec((ts, LANES), lambda b, p, pt: (rb + b, 0))],
        out_specs=pl.BlockSpec((ts, q_cols), lambda b, p, pt: (b, 0)),
        scratch_shapes=[pltpu.VMEM((N_KV_HEADS, GROUP, ts, 1), F32),
                        pltpu.VMEM((N_KV_HEADS, GROUP, ts, 1), F32),
                        pltpu.VMEM((N_KV_HEADS, GROUP, ts, HEAD_DIM), F32)],
    )
    return pl.pallas_call(
        kern,
        grid_spec=grid_spec,
        out_shape=jax.ShapeDtypeStruct((bs * ts, q_cols), F32),
        compiler_params=_params(2),
    )(page_table, *([cache_sel] * PAGES_PER_STEP), q, new_sel, selmask, expand, win_buf, new_win, o_c, gates)


def _gate_weight(w_in, layer):
    wg = w_in[layer:layer + 1, :, N_HEADS * HEAD_DIM + 3 * KV_COLS:]
    return jnp.pad(wg, ((0, 0), (0, 0), (0, LANES - wg.shape[2])))


def _attn_layer(x, layer, shapes, cache_cmp, cache_sel, n_pool, win_buf, page_table,
                norm, w_in, pool_w, pe, w1, w2, w_out):
    n_batch, seq, bs, ts, past = shapes
    m_prompt = n_batch * seq
    q_cols = N_HEADS * HEAD_DIM
    h = _rmsnorm(x, norm, BF16)
    q = _matmul(h, w_in, layer, q_cols, scale=SCALE)
    kv_rows, kv_heads = _matmul_kv(h, w_in, layer, q_cols)
    gates = _matmul(h, _gate_weight(w_in, layer), 0, LANES)

    ha, hb = _halfsums_prompt(kv_rows, pool_w, m_prompt)
    half_rows = seq // CMP_STRIDE * KV_SLOTS
    ckv_p = _cmp_mlp(ha.reshape(n_batch, half_rows, HEAD_DIM), hb.reshape(n_batch, half_rows, HEAD_DIM),
                     pool_w, pe, w1, w2)
    mix_p = _nsa_prompt(q, kv_heads, gates, ckv_p, n_batch, seq)

    sa, sb = _halfsums_paged(cache_cmp, layer, n_pool, page_table, pool_w)
    ckv_s = _cmp_mlp(sa, sb, pool_w, pe, w1, w2)
    o_c, selmask = _nsa_sample_cmp(q, ckv_s, m_prompt, bs, ts, past)
    mix_s = _nsa_sample_attn(q, kv_rows, gates, cache_sel, layer, n_pool, page_table, selmask, win_buf, o_c,
                             m_prompt, bs, ts, past)

    mixed = jnp.concatenate([mix_p, mix_s.astype(BF16)], axis=0)
    x = _matmul_residual(mixed, w_out, layer, x)

    def kv_p(branch):
        return kv_rows[branch, :m_prompt * KV_SLOTS].reshape(n_batch, seq, N_KV_HEADS, 2, HEAD_DIM)

    def kv_s(branch):
        return kv_rows[branch, m_prompt * KV_SLOTS:].reshape(bs, ts, N_KV_HEADS, 2, HEAD_DIM)

    n_buf = win_buf.shape[1] // KV_SLOTS
    win_s = jnp.concatenate([win_buf.reshape(bs, n_buf, N_KV_HEADS, 2, HEAD_DIM), kv_s(2)], axis=1)
    new = (kv_p(0), kv_p(1), kv_p(2)[:, -min(WINDOW, seq):], kv_s(0), kv_s(1), win_s[:, -n_buf:])
    return x, new


def _conv_layer(x, layer, shapes, state, norm, w_in, conv_k, w_out):
    n_batch, seq, bs, ts, _ = shapes
    m_prompt = n_batch * seq
    d = w_out.shape[1]
    h = _rmsnorm(x, norm, BF16)
    bgate, u = _matmul_conv_in(h, w_in, layer)
    k_pad = jnp.pad(conv_k, ((0, SUBLANES - CONV_W), (0, 0)))
    hist_p = jnp.zeros((n_batch, SUBLANES, d), F32)
    hist_s = jnp.pad(state, ((0, 0), (SUBLANES - (CONV_W - 1), 0), (0, 0)))
    z_p = _conv_apply(u, bgate, hist_p, k_pad, 0, n_batch, seq, BF16)
    z_s = _conv_apply(u, bgate, hist_s, k_pad, m_prompt, bs, ts, F32)
    z = jnp.concatenate([z_p, z_s.astype(BF16)], axis=0)
    x = _matmul_residual(z, w_out, layer, x)
    tail = CONV_W - 1
    new_p = jnp.stack([u[(b + 1) * seq - tail:(b + 1) * seq] for b in range(n_batch)])
    new_s = u[m_prompt:].reshape(bs, ts, d)[:, -tail:]
    return x, new_p, new_s


def _ffn(x, layer, norm, w_in, w_out):
    h = _rmsnorm(x, norm, BF16)
    hid = _matmul_swiglu(h, w_in, layer)
    return _matmul_residual(hid, w_out, layer, x)


def kernel(x_prompt, x_sample, cache_cmp_kv, cache_sel_kv, state_win_kv, state_conv, page_table, attn_norm, attn_w_in, attn_cmp_pool, attn_cmp_pe, attn_cmp_w1, attn_cmp_w2, attn_w_out, conv_norm, conv_w_in, conv_kernel, conv_w_out, ffn_norm, ffn_w_in, ffn_w_out, final_norm):
    n_batch, seq, d = x_prompt.shape
    bs, ts, _ = x_sample.shape
    past = page_table.shape[1] * PAGE_SIZE
    n_buf = state_win_kv.shape[2]
    depth = ffn_norm.shape[0]
    assert d == N_HEADS * HEAD_DIM and seq % SEL_SPAN == 0
    assert ts == SUBLANES and (n_batch * seq) % ts == 0
    shapes = (n_batch, seq, bs, ts, past)
    m_prompt = n_batch * seq
    n_attn, n_pool = cache_cmp_kv.shape[:2]
    cache_cmp = cache_cmp_kv.reshape(n_attn * n_pool, PAGE_SIZE * KV_SLOTS, HEAD_DIM)
    cache_sel = cache_sel_kv.reshape(n_attn * n_pool, PAGE_SIZE * KV_SLOTS, HEAD_DIM)

    x = jnp.concatenate([x_prompt.reshape(m_prompt, d), x_sample.reshape(bs * ts, d)], axis=0)
    attn_new, conv_new = [], []
    for i in range(depth):
        l = i // 2
        if i % 2 == 0:
            win_buf = state_win_kv[l].reshape(bs, n_buf * KV_SLOTS, HEAD_DIM)
            x, new = _attn_layer(x, l, shapes, cache_cmp, cache_sel, n_pool, win_buf, page_table,
                                 attn_norm[l], attn_w_in, attn_cmp_pool[l], attn_cmp_pe[l],
                                 attn_cmp_w1[l], attn_cmp_w2[l], attn_w_out)
            attn_new.append(new)
        else:
            x, new_p, new_s = _conv_layer(x, l, shapes, state_conv[l], conv_norm[l], conv_w_in,
                                          conv_kernel[l], conv_w_out)
            conv_new.append((new_p, new_s))
        x = _ffn(x, i, ffn_norm[i], ffn_w_in, ffn_w_out)
    y = _rmsnorm(x, final_norm, F32)
    stack = lambda k: jnp.stack([a[k] for a in attn_new])
    return (y[:m_prompt].reshape(n_batch, seq, d), y[m_prompt:].reshape(bs, ts, d),
            stack(0), stack(1), stack(2), jnp.stack([c[0] for c in conv_new]),
            stack(3), stack(4), stack(5), jnp.stack([c[1] for c in conv_new]))
```

```python
import functools

import numpy as np
import jax
import jax.numpy as jnp
from jax import lax
from jax.experimental import pallas as pl
from jax.experimental.pallas import tpu as pltpu

N_HEADS = 16
HEAD_DIM = 128
N_KV_HEADS = 4
GROUP = N_HEADS // N_KV_HEADS
KV_SLOTS = N_KV_HEADS * 2
KV_COLS = KV_SLOTS * HEAD_DIM
CMP_BLOCK = 32
CMP_STRIDE = 16
SEL_BLOCK = 64
SEL_SHIFT = 6
TOP_N = 16
N_LOCAL_FORCED = 2
WINDOW = 512
PAGE_SIZE = 128
CONV_W = 3
EPS = 1e-6
SCALE = HEAD_DIM ** -0.5
NEG = -1e30

LANES = 128
SUBLANES = 8
SEL_CHUNK = 512
SEL_SPAN = 512
Q_TILE = 64
PRE_TILE = 256
PAGES_PER_STEP = 16
VMEM_LIMIT = 56 * 1024 * 1024

F32 = jnp.float32
BF16 = jnp.bfloat16


def _params(n_axes):
    return pltpu.CompilerParams(dimension_semantics=("arbitrary",) * n_axes,
                                vmem_limit_bytes=VMEM_LIMIT)


def _pick_tile(n, limit, mult):
    best = None
    for t in range(mult, min(n, limit) + 1, mult):
        if n % t == 0:
            best = t
    assert best is not None, (n, limit, mult)
    return best


def _dot(a, b):
    return jnp.dot(a, b, preferred_element_type=F32)


def _dot_nt(a, b):
    return lax.dot_general(a, b, (((1,), (1,)), ((), ())), preferred_element_type=F32)


def _silu(x):
    return x * jax.nn.sigmoid(x)


def _slot_rows(ref, slot, n_rows):
    return ref[pl.ds(slot, n_rows, stride=KV_SLOTS), :]


def _rmsnorm_kernel(x_ref, g_ref, o_ref):
    x = x_ref[...]
    ms = jnp.mean(x * x, axis=-1, keepdims=True)
    o_ref[...] = ((x * lax.rsqrt(ms + EPS)) * g_ref[...]).astype(o_ref.dtype)


def _rmsnorm(x, gain, out_dtype):
    m, d = x.shape
    tm = _pick_tile(m, 1024, 16)
    return pl.pallas_call(
        _rmsnorm_kernel,
        grid=(m // tm,),
        in_specs=[pl.BlockSpec((tm, d), lambda i: (i, 0)),
                  pl.BlockSpec((1, d), lambda i: (0, 0))],
        out_specs=pl.BlockSpec((tm, d), lambda i: (i, 0)),
        out_shape=jax.ShapeDtypeStruct((m, d), out_dtype),
        compiler_params=_params(1),
    )(x, gain.reshape(1, d))


def _cast_weights_once(w_refs, w_scrs):
    @pl.when(pl.program_id(1) == 0)
    def _():
        for w_ref, w_scr in zip(w_refs, w_scrs):
            w_scr[...] = w_ref[...].astype(BF16)


W_BLOCK_BYTES = 16 * 1024 * 1024


def _w_spec(k, tn, layer, blk0=0):
    single = 2 * 4 * k * tn > W_BLOCK_BYTES
    return pl.BlockSpec((None, k, tn), lambda j, i: (layer, 0, blk0 + j),
                        pipeline_mode=pl.Buffered(1) if single else None)


def _wide_tile(n, k):
    return max(t for t in (LANES, 2 * LANES, 4 * LANES, 8 * LANES) if n % t == 0 and 4 * k * t <= W_BLOCK_BYTES)


def _w_scratch(k, tn, count=1):
    return [pltpu.VMEM((k, tn), BF16) for _ in range(count)]


def _mm_kernel(a_ref, w_ref, o_ref, w_scr, *, scale):
    _cast_weights_once([w_ref], [w_scr])
    acc = _dot(a_ref[...], w_scr[...])
    o_ref[...] = acc if scale is None else acc * scale


def _matmul(a, w, layer, n, scale=None):
    m, k = a.shape
    tm = _pick_tile(m, 1024, 16)
    tn = _wide_tile(n, k)
    return pl.pallas_call(
        functools.partial(_mm_kernel, scale=scale),
        grid=(n // tn, m // tm),
        in_specs=[pl.BlockSpec((tm, k), lambda j, i: (i, 0)), _w_spec(k, tn, layer)],
        out_specs=pl.BlockSpec((tm, tn), lambda j, i: (i, j)),
        out_shape=jax.ShapeDtypeStruct((m, n), F32),
        scratch_shapes=_w_scratch(k, tn),
        compiler_params=_params(2),
    )(a, w)


def _mm_kv_kernel(a_ref, w_ref, rows_ref, heads_ref, w_scr):
    _cast_weights_once([w_ref], [w_scr])
    acc = _dot(a_ref[...], w_scr[...])
    tm = acc.shape[0]
    for c in range(KV_SLOTS):
        part = acc[:, c * HEAD_DIM:(c + 1) * HEAD_DIM]
        rows_ref[pl.ds(c, tm, stride=KV_SLOTS), :] = part
        heads_ref[c] = part.astype(heads_ref.dtype)


def _matmul_kv(a, w, layer, col0):
    m, k = a.shape
    tm = _pick_tile(m, 1024, 16)
    return pl.pallas_call(
        _mm_kv_kernel,
        grid=(3, m // tm),
        in_specs=[pl.BlockSpec((tm, k), lambda j, i: (i, 0)), _w_spec(k, KV_COLS, layer, col0 // KV_COLS)],
        out_specs=[pl.BlockSpec((None, tm * KV_SLOTS, HEAD_DIM), lambda j, i: (j, i, 0)),
                   pl.BlockSpec((None, KV_SLOTS, tm, HEAD_DIM), lambda j, i: (j, 0, i, 0))],
        out_shape=[jax.ShapeDtypeStruct((3, m * KV_SLOTS, HEAD_DIM), F32),
                   jax.ShapeDtypeStruct((3, KV_SLOTS, m, HEAD_DIM), BF16)],
        scratch_shapes=_w_scratch(k, KV_COLS),
        compiler_params=_params(2),
    )(a, w)


def _mm_res_kernel(a_ref, w_ref, r_ref, o_ref, w_scr):
    _cast_weights_once([w_ref], [w_scr])
    o_ref[...] = r_ref[...] + _dot(a_ref[...], w_scr[...])


def _matmul_residual(a, w, layer, res):
    m, k = a.shape
    n = w.shape[2]
    tm = _pick_tile(m, 1024, 16)
    tn = _wide_tile(n, k)
    return pl.pallas_call(
        _mm_res_kernel,
        grid=(n // tn, m // tm),
        in_specs=[pl.BlockSpec((tm, k), lambda j, i: (i, 0)), _w_spec(k, tn, layer),
                  pl.BlockSpec((tm, tn), lambda j, i: (i, j))],
        out_specs=pl.BlockSpec((tm, tn), lambda j, i: (i, j)),
        out_shape=jax.ShapeDtypeStruct((m, n), F32),
        scratch_shapes=_w_scratch(k, tn),
        compiler_params=_params(2),
    )(a, w, res)


def _mm_swiglu_kernel(a_ref, wg_ref, wu_ref, o_ref, wg_scr, wu_scr):
    _cast_weights_once([wg_ref, wu_ref], [wg_scr, wu_scr])
    a = a_ref[...]
    gate = _dot(a, wg_scr[...])
    up = _dot(a, wu_scr[...])
    o_ref[...] = (_silu(gate) * up).astype(o_ref.dtype)


def _matmul_swiglu(a, w, layer):
    m, k = a.shape
    f = w.shape[2] // 2
    tm = _pick_tile(m, 1024, 16)
    tn = _pick_tile(f, 512, LANES)
    nb = f // tn
    return pl.pallas_call(
        _mm_swiglu_kernel,
        grid=(nb, m // tm),
        in_specs=[pl.BlockSpec((tm, k), lambda j, i: (i, 0)), _w_spec(k, tn, layer), _w_spec(k, tn, layer, nb)],
        out_specs=pl.BlockSpec((tm, tn), lambda j, i: (i, j)),
        out_shape=jax.ShapeDtypeStruct((m, f), BF16),
        scratch_shapes=_w_scratch(k, tn, 2),
        compiler_params=_params(2),
    )(a, w, w)


def _mm_conv_in_kernel(a_ref, wb_ref, wc_ref, wx_ref, b_ref, u_ref, wb_scr, wc_scr, wx_scr):
    _cast_weights_once([wb_ref, wc_ref, wx_ref], [wb_scr, wc_scr, wx_scr])
    a = a_ref[...]
    b_ref[...] = _dot(a, wb_scr[...])
    u_ref[...] = _dot(a, wc_scr[...]) * _dot(a, wx_scr[...])


def _matmul_conv_in(a, w, layer):
    m, k = a.shape
    dc = w.shape[2] // 3
    tm = _pick_tile(m, 1024, 16)
    tn = _pick_tile(dc, 512, LANES)
    nb = dc // tn
    out = jax.ShapeDtypeStruct((m, dc), F32)
    return pl.pallas_call(
        _mm_conv_in_kernel,
        grid=(nb, m // tm),
        in_specs=[pl.BlockSpec((tm, k), lambda j, i: (i, 0)),
                  _w_spec(k, tn, layer), _w_spec(k, tn, layer, nb), _w_spec(k, tn, layer, 2 * nb)],
        out_specs=[pl.BlockSpec((tm, tn), lambda j, i: (i, j)),
                   pl.BlockSpec((tm, tn), lambda j, i: (i, j))],
        out_shape=[out, out],
        scratch_shapes=_w_scratch(k, tn, 3),
        compiler_params=_params(2),
    )(a, w, w, w)


def _conv_apply_kernel(u_ref, b_ref, prev_ref, hist_ref, k_ref, o_ref):
    i = pl.program_id(1)
    u = u_ref[...]
    rows = u.shape[0]
    tail = jnp.where(i == 0, hist_ref[...], prev_ref[...])
    row = lax.broadcasted_iota(jnp.int32, u.shape, 0)
    u1 = jnp.where(row == 0, tail[7:8], pltpu.roll(u, 1, axis=0))
    u2 = jnp.where(row == 0, tail[6:7], jnp.where(row == 1, tail[7:8], pltpu.roll(u, 2 % rows, axis=0)))
    k = k_ref[...]
    v = k[0:1] * u2 + k[1:2] * u1 + k[2:3] * u
    o_ref[...] = (b_ref[...] * v).astype(o_ref.dtype)


def _conv_apply(u, bgate, hist, conv_k, row0, n_seq, seq_len, out_dtype):
    d = u.shape[1]
    tt = _pick_tile(seq_len, 512, SUBLANES)
    tc = _pick_tile(d, 1024, LANES)
    tiles = seq_len // tt
    blk0 = row0 // tt
    sub = tt // SUBLANES

    def cur(s, i, j):
        return (blk0 + s * tiles + i, j)

    def prev(s, i, j):
        return (jnp.maximum((blk0 + s * tiles + i) * sub - 1, 0), j)

    return pl.pallas_call(
        _conv_apply_kernel,
        grid=(n_seq, tiles, d // tc),
        in_specs=[pl.BlockSpec((tt, tc), cur),
                  pl.BlockSpec((tt, tc), cur),
                  pl.BlockSpec((SUBLANES, tc), prev),
                  pl.BlockSpec((None, SUBLANES, tc), lambda s, i, j: (s, 0, j)),
                  pl.BlockSpec((SUBLANES, tc), lambda s, i, j: (0, j))],
        out_specs=pl.BlockSpec((tt, tc), lambda s, i, j: (s * tiles + i, j)),
        out_shape=jax.ShapeDtypeStruct((n_seq * seq_len, d), out_dtype),
        compiler_params=_params(3),
    )(u, bgate, u, hist, conv_k)


def _chunk_halfsums(x, wa, wb):
    n = x.shape[0] // (CMP_STRIDE * KV_SLOTS)
    x4 = x.reshape(n, CMP_STRIDE, KV_SLOTS, HEAD_DIM)
    a = jnp.sum(x4 * wa[None], axis=1).reshape(n * KV_SLOTS, HEAD_DIM)
    b = jnp.sum(x4 * wb[None], axis=1).reshape(n * KV_SLOTS, HEAD_DIM)
    return a, b


def _halfsum_kernel(x_ref, wa_ref, wb_ref, a_ref, b_ref):
    a_ref[...], b_ref[...] = _chunk_halfsums(x_ref[...], wa_ref[...], wb_ref[...])


def _halfsum_paged_kernel(pt_ref, *refs):
    del pt_ref
    page_refs = refs[:PAGES_PER_STEP]
    wa_ref, wb_ref, a_ref, b_ref = refs[PAGES_PER_STEP:]
    rows = (PAGE_SIZE // CMP_STRIDE) * KV_SLOTS
    for k, page_ref in enumerate(page_refs):
        a, b = _chunk_halfsums(page_ref[...], wa_ref[...], wb_ref[...])
        a_ref[k * rows:(k + 1) * rows, :] = a
        b_ref[k * rows:(k + 1) * rows, :] = b


def _pool_weight_maps(pool_w):
    w = pool_w.reshape(2, CMP_BLOCK // CMP_STRIDE, CMP_STRIDE)
    maps = []
    for half in range(2):
        wj = jnp.tile(w[:, half, :].T, (1, N_KV_HEADS))
        maps.append(jnp.broadcast_to(wj[:, :, None], (CMP_STRIDE, KV_SLOTS, HEAD_DIM)))
    return maps


def _halfsums_prompt(kv_rows, pool_w, n_rows):
    wa, wb = _pool_weight_maps(pool_w)
    tr = _pick_tile(n_rows, 512, LANES)
    out_rows = tr // CMP_STRIDE * KV_SLOTS
    out = jax.ShapeDtypeStruct((n_rows // CMP_STRIDE * KV_SLOTS, HEAD_DIM), F32)
    wspec = pl.BlockSpec((CMP_STRIDE, KV_SLOTS, HEAD_DIM), lambda i: (0, 0, 0))
    ospec = pl.BlockSpec((out_rows, HEAD_DIM), lambda i: (i, 0))
    return pl.pallas_call(
        _halfsum_kernel,
        grid=(n_rows // tr,),
        in_specs=[pl.BlockSpec((None, tr * KV_SLOTS, HEAD_DIM), lambda i: (0, i, 0)), wspec, wspec],
        out_specs=[ospec, ospec],
        out_shape=[out, out],
        compiler_params=_params(1),
    )(kv_rows, wa, wb)


def _page_specs(layer, n_pool, n_pages):
    def spec(k):
        def index(b, p, pt):
            page = jnp.minimum(p * PAGES_PER_STEP + k, n_pages - 1)
            return (layer * n_pool + pt[b, page], 0, 0)
        return pl.BlockSpec((None, PAGE_SIZE * KV_SLOTS, HEAD_DIM), index)
    return [spec(k) for k in range(PAGES_PER_STEP)]


def _halfsums_paged(cache, layer, n_pool, page_table, pool_w):
    bs, n_pages = page_table.shape
    assert n_pages % PAGES_PER_STEP == 0
    wa, wb = _pool_weight_maps(pool_w)
    rows = PAGES_PER_STEP * (PAGE_SIZE // CMP_STRIDE) * KV_SLOTS
    out = jax.ShapeDtypeStruct((bs, n_pages * (PAGE_SIZE // CMP_STRIDE) * KV_SLOTS, HEAD_DIM), F32)
    wspec = pl.BlockSpec((CMP_STRIDE, KV_SLOTS, HEAD_DIM), lambda b, p, pt: (0, 0, 0))
    ospec = pl.BlockSpec((None, rows, HEAD_DIM), lambda b, p, pt: (b, p, 0))
    grid_spec = pltpu.PrefetchScalarGridSpec(
        num_scalar_prefetch=1,
        grid=(bs, n_pages // PAGES_PER_STEP),
        in_specs=_page_specs(layer, n_pool, n_pages) + [wspec, wspec],
        out_specs=[ospec, ospec],
    )
    return pl.pallas_call(
        _halfsum_paged_kernel,
        grid_spec=grid_spec,
        out_shape=[out, out],
        compiler_params=_params(2),
    )(page_table, *([cache] * PAGES_PER_STEP), wa, wb)


def _cmp_mlp_kernel(a_ref, b_ref, pw_ref, pe_ref, w1_ref, w2_ref, o_ref):
    n = o_ref.shape[1]
    for k in range(2):
        pe_term = jnp.sum(pw_ref[k] * pe_ref[k], axis=0, keepdims=True)
        w1 = w1_ref[k].astype(BF16)
        w2 = w2_ref[k].astype(BF16)
        for g in range(N_KV_HEADS):
            c = 2 * g + k
            x = _slot_rows(a_ref, c, n) + pltpu.roll(_slot_rows(b_ref, c, n), n - 1, axis=0) + pe_term
            hid = _silu(_dot(x.astype(BF16), w1))
            o_ref[c] = _dot(hid.astype(BF16), w2)


def _cmp_mlp(half_a, half_b, pool_w, pe, w1, w2):
    n_seq, rows, _ = half_a.shape
    n_chunks = rows // KV_SLOTS
    spec = pl.BlockSpec((None, rows, HEAD_DIM), lambda b: (b, 0, 0))
    full3 = lambda shape: pl.BlockSpec(shape, lambda b: (0, 0, 0))
    return pl.pallas_call(
        _cmp_mlp_kernel,
        grid=(n_seq,),
        in_specs=[spec, spec, full3((2, CMP_BLOCK, 1)), full3((2, CMP_BLOCK, HEAD_DIM)),
                  full3((2, HEAD_DIM, HEAD_DIM)), full3((2, HEAD_DIM, HEAD_DIM))],
        out_specs=pl.BlockSpec((None, KV_SLOTS, n_chunks, HEAD_DIM), lambda b: (b, 0, 0, 0)),
        out_shape=jax.ShapeDtypeStruct((n_seq, KV_SLOTS, n_chunks, HEAD_DIM), F32),
        compiler_params=_params(1),
    )(half_a, half_b, pool_w.reshape(2, CMP_BLOCK, 1), pe, w1, w2)


def _overlap_matrix(n_cmp_pad, n_sel_pad):
    c_start = np.arange(n_cmp_pad)[:, None] * CMP_STRIDE
    b_start = np.arange(n_sel_pad)[None, :] * SEL_BLOCK
    return (c_start < b_start + SEL_BLOCK) & (c_start + CMP_BLOCK > b_start)


def _stack_heads(q):
    return jnp.concatenate([q[:, r * HEAD_DIM:(r + 1) * HEAD_DIM] for r in range(GROUP)], axis=0)


def _cmp_attend(q4, kc, vc, tpos):
    t = tpos.shape[0]
    n_cmp = kc.shape[0]
    s = _dot_nt(q4, kc).reshape(GROUP, t, n_cmp)
    last = lax.broadcasted_iota(jnp.int32, (1, n_cmp), 1) * CMP_STRIDE + (CMP_BLOCK - 1)
    ok = (last <= tpos)[None]
    sm = jnp.where(ok, s, NEG)
    m = jnp.max(sm, axis=-1, keepdims=True)
    e = jnp.where(ok, jnp.exp(sm - m), 0.0)
    p = e / jnp.maximum(jnp.sum(e, axis=-1, keepdims=True), 1e-30)
    pb = p.astype(BF16).reshape(GROUP * t, n_cmp)
    o_c = _dot(pb, vc).reshape(GROUP, t, HEAD_DIM)
    return o_c, pb


def _block_scores(imp, blk, tpos, n_sel):
    cur = tpos >> SEL_SHIFT
    real = blk < n_sel
    valid = (blk * SEL_BLOCK <= tpos) & real
    forced = ((blk == 0) | ((blk <= cur) & (blk > cur - N_LOCAL_FORCED))) & real
    return jnp.where(forced, jnp.inf, jnp.where(valid, imp, -jnp.inf))


def _masked_online_step(q4, k, v, bias, carry):
    m_i, l_i, acc = carry
    t, n_keys = bias.shape
    s = _dot_nt(q4, k).reshape(GROUP, t, n_keys) + bias[None]
    m_new = jnp.maximum(m_i, jnp.max(s, axis=-1, keepdims=True))
    alpha = jnp.exp(m_i - m_new)
    p = jnp.exp(s - m_new)
    l_new = alpha * l_i + jnp.sum(p, axis=-1, keepdims=True)
    pv = _dot(p.astype(BF16).reshape(GROUP * t, n_keys), v).reshape(GROUP, t, HEAD_DIM)
    return m_new, l_new, alpha * acc + pv


def _online_init(t):
    return (jnp.full((GROUP, t, 1), NEG, F32), jnp.zeros((GROUP, t, 1), F32),
            jnp.zeros((GROUP, t, HEAD_DIM), F32))


def _fold_lane_tiles(x, op):
    out = x[..., 0:LANES]
    for j in range(1, x.shape[-1] // LANES):
        out = op(out, x[..., j * LANES:(j + 1) * LANES])
    return out


def _gated_mix(gate, lane0, o_c, o_s, o_w):
    outs = []
    for r in range(GROUP):
        lanes = [lane0 + branch * N_HEADS + r for branch in range(3)]
        outs.append(gate[:, lanes[0]:lanes[0] + 1] * o_c[r]
                    + gate[:, lanes[1]:lanes[1] + 1] * o_s[r]
                    + gate[:, lanes[2]:lanes[2] + 1] * o_w[r])
    return outs


def _topk_mask_by_rank(score_t, k, n_live):
    n, lanes = score_t.shape
    tiles = [score_t[v * SUBLANES:(v + 1) * SUBLANES] for v in range(n_live // SUBLANES)]
    ranks = [jnp.zeros((SUBLANES, lanes), F32) for _ in tiles]
    row = lax.broadcasted_iota(jnp.int32, (SUBLANES, 1), 0)
    for i in range(n_live):
        si = score_t[i:i + 1, :]
        for v, tile in enumerate(tiles):
            lo = v * SUBLANES
            if lo > i:
                ahead = jnp.where(si >= tile, 1.0, 0.0)
            elif lo + SUBLANES - 1 < i:
                ahead = jnp.where(si > tile, 1.0, 0.0)
            else:
                tie = jnp.where(row + lo > i, 1.0, 0.0)
                ahead = jnp.where(si > tile, 1.0, jnp.where(si == tile, tie, 0.0))
            ranks[v] = ranks[v] + ahead
    live = [jnp.where(tile > -jnp.inf, jnp.where(rank < k, 1.0, 0.0), 0.0) for tile, rank in zip(tiles, ranks)]
    dead = [jnp.zeros((n - n_live, lanes), F32)] if n > n_live else []
    return jnp.concatenate(live + dead, axis=0)


def _head_gate_lanes(gate_ref):
    return pltpu.roll(jax.nn.sigmoid(gate_ref[...]), LANES - GROUP * pl.program_id(1), axis=1)


def _nsa_prompt_pre_kernel(q_ref, kw_ref, vw_ref, kc_ref, vc_ref, gate_ref, ov_ref, part_ref, score_ref):
    c = pl.program_id(2)
    tq = q_ref.shape[0]
    n_sel = ov_ref.shape[0]
    start = c * tq
    q4 = _stack_heads(q_ref[...]).astype(BF16)
    tpos = start + lax.broadcasted_iota(jnp.int32, (tq, 1), 0)

    o_c, pb = _cmp_attend(q4, kc_ref[...].astype(BF16), vc_ref[...].astype(BF16), tpos)

    imp_all = _dot_nt(ov_ref[...], pb)
    imp_t = imp_all[:, 0:tq]
    for r in range(1, GROUP):
        imp_t = imp_t + imp_all[:, r * tq:(r + 1) * tq]
    tpos_t = start + lax.broadcasted_iota(jnp.int32, (1, tq), 1)
    blk_t = lax.broadcasted_iota(jnp.int32, (n_sel, 1), 0)
    score_ref[...] = _block_scores(imp_t, blk_t, tpos_t, n_sel)

    win_keys = WINDOW + tq
    lo = pl.multiple_of(jnp.maximum(start - WINDOW, 0), LANES)
    rel = tpos - (lo + lax.broadcasted_iota(jnp.int32, (1, win_keys), 1))
    bias_w = jnp.where((rel >= 0) & (rel < WINDOW), 0.0, NEG)
    _, l_w, acc_w = _masked_online_step(q4, kw_ref[pl.ds(lo, win_keys), :], vw_ref[pl.ds(lo, win_keys), :],
                                        bias_w, _online_init(tq))
    o_w = acc_w / l_w

    gate = _head_gate_lanes(gate_ref)
    part_ref[...] = jnp.concatenate(
        [gate[:, r:r + 1] * o_c[r] + gate[:, 2 * N_HEADS + r:2 * N_HEADS + r + 1] * o_w[r] for r in range(GROUP)],
        axis=1)


def _nsa_prompt_sel_kernel(q_ref, ks_ref, vs_ref, score_ref, gate_ref, part_ref, exp_ref, o_ref,
                           sel_scr, os_scr, s_scr):
    c = pl.program_id(2)
    tq = Q_TILE
    seq = vs_ref.shape[0]
    n_sel = score_ref.shape[0]
    start = c * tq
    q4 = _stack_heads(q_ref[...]).astype(BF16)
    tpos = start + lax.broadcasted_iota(jnp.int32, (tq, 1), 0)
    qb_per_span = SEL_SPAN // tq

    def in_span(v):
        return (c >= v * qb_per_span) & (c < (v + 1) * qb_per_span)

    for v in range(seq // SEL_SPAN):
        @pl.when(in_span(v) & ((c & 1) == 0))
        def _():
            sel_t = _topk_mask_by_rank(score_ref[...], min(TOP_N, n_sel), (v + 1) * SEL_SPAN // SEL_BLOCK)
            sel_sq = jnp.concatenate([sel_t, jnp.zeros((LANES - n_sel, LANES), F32)], axis=0)
            sel_scr[...] = sel_sq.T

    selb = sel_scr[pl.ds(pl.multiple_of((c & 1) * tq, tq), tq), :].astype(BF16)

    for v in range(seq // SEL_SPAN):
        @pl.when(in_span(v))
        def _():
            n_keys = (v + 1) * SEL_SPAN
            m_lane = jnp.full((GROUP, tq, LANES), NEG, F32)
            for k in range(n_keys // SEL_CHUNK):
                off = k * SEL_CHUNK
                picked = _dot(selb, exp_ref[k])
                kpos = off + lax.broadcasted_iota(jnp.int32, (1, SEL_CHUNK), 1)
                bias = jnp.where(picked > 0.5, jnp.where(kpos <= tpos, 0.0, NEG), NEG)
                s_k = _dot(q4, ks_ref[:, off:off + SEL_CHUNK]).reshape(GROUP, tq, SEL_CHUNK) + bias[None]
                s_scr[k] = s_k
                m_lane = jnp.maximum(m_lane, _fold_lane_tiles(s_k, jnp.maximum))
            m_s = jnp.max(m_lane, axis=-1, keepdims=True)
            l_lane = jnp.zeros((GROUP, tq, LANES), F32)
            acc = jnp.zeros((GROUP * tq, HEAD_DIM), F32)
            for k in range(n_keys // SEL_CHUNK):
                off = k * SEL_CHUNK
                p_k = jnp.exp(s_scr[k] - m_s)
                l_lane = l_lane + _fold_lane_tiles(p_k, jnp.add)
                acc = acc + _dot(p_k.astype(BF16).reshape(GROUP * tq, SEL_CHUNK), vs_ref[off:off + SEL_CHUNK, :])
            os_scr[...] = acc.reshape(GROUP, tq, HEAD_DIM) / jnp.sum(l_lane, axis=-1, keepdims=True)

    gate = _head_gate_lanes(gate_ref)
    o_s = os_scr[...]
    mix = jnp.concatenate([gate[:, N_HEADS + r:N_HEADS + r + 1] * o_s[r] for r in range(GROUP)], axis=1)
    o_ref[...] = (part_ref[...] + mix).astype(o_ref.dtype)


def _nsa_prompt(q, kv_heads, gates, ckv, n_batch, seq):
    n_qb = seq // Q_TILE
    n_pre = seq // PRE_TILE
    n_cmp = seq // CMP_STRIDE
    n_sel = seq // SEL_BLOCK
    n_chunks = seq // SEL_CHUNK
    qw = GROUP * HEAD_DIM
    assert 2 * Q_TILE == LANES and n_sel <= LANES and n_sel % SUBLANES == 0
    assert seq % PRE_TILE == 0 and PRE_TILE % LANES == 0 and seq >= WINDOW + PRE_TILE
    overlap_t = jnp.asarray(_overlap_matrix(n_cmp, n_sel).T, BF16)
    key_blk = np.arange(seq) // SEL_BLOCK
    expand = np.arange(LANES)[:, None] == key_blk[None, :]
    expand = jnp.asarray(expand.reshape(LANES, n_chunks, SEL_CHUNK).transpose(1, 0, 2), BF16)

    def kv_spec(branch, kv):
        return pl.BlockSpec((None, None, seq, HEAD_DIM), lambda b, g, c: (branch, 2 * g + kv, b, 0))

    def ckv_spec(kv):
        return pl.BlockSpec((None, None, n_cmp, HEAD_DIM), lambda b, g, c: (b, 2 * g + kv, 0, 0))

    part, score = pl.pallas_call(
        _nsa_prompt_pre_kernel,
        grid=(n_batch, N_KV_HEADS, n_pre),
        in_specs=[pl.BlockSpec((PRE_TILE, qw), lambda b, g, c: (b * n_pre + c, g)),
                  kv_spec(2, 0), kv_spec(2, 1), ckv_spec(0), ckv_spec(1),
                  pl.BlockSpec((PRE_TILE, LANES), lambda b, g, c: (b * n_pre + c, 0)),
                  pl.BlockSpec((n_sel, n_cmp), lambda b, g, c: (0, 0))],
        out_specs=[pl.BlockSpec((PRE_TILE, qw), lambda b, g, c: (b * n_pre + c, g)),
                   pl.BlockSpec((None, None, n_sel, PRE_TILE), lambda b, g, c: (b, g, 0, c))],
        out_shape=[jax.ShapeDtypeStruct((n_batch * seq, N_HEADS * HEAD_DIM), F32),
                   jax.ShapeDtypeStruct((n_batch, N_KV_HEADS, n_sel, seq), F32)],
        compiler_params=_params(3),
    )(q, kv_heads, kv_heads, ckv, ckv, gates, overlap_t)

    k_sel_t = jnp.swapaxes(kv_heads[1, 0::2], 1, 2)
    return pl.pallas_call(
        _nsa_prompt_sel_kernel,
        grid=(n_batch, N_KV_HEADS, n_qb),
        in_specs=[pl.BlockSpec((Q_TILE, qw), lambda b, g, c: (b * n_qb + c, g)),
                  pl.BlockSpec((None, HEAD_DIM, seq), lambda b, g, c: (g, 0, b)), kv_spec(1, 1),
                  pl.BlockSpec((None, None, n_sel, LANES), lambda b, g, c: (b, g, 0, c // 2)),
                  pl.BlockSpec((Q_TILE, LANES), lambda b, g, c: (b * n_qb + c, 0)),
                  pl.BlockSpec((Q_TILE, qw), lambda b, g, c: (b * n_qb + c, g)),
                  pl.BlockSpec((n_chunks, LANES, SEL_CHUNK), lambda b, g, c: (0, 0, 0))],
        out_specs=pl.BlockSpec((Q_TILE, qw), lambda b, g, c: (b * n_qb + c, g)),
        out_shape=jax.ShapeDtypeStruct((n_batch * seq, N_HEADS * HEAD_DIM), BF16),
        scratch_shapes=[pltpu.VMEM((LANES, LANES), F32),
                        pltpu.VMEM((GROUP, Q_TILE, HEAD_DIM), F32),
                        pltpu.VMEM((n_chunks, GROUP, Q_TILE, SEL_CHUNK), F32)],
        compiler_params=_params(3),
    )(q, k_sel_t, kv_heads, score, gates, part, expand)


def _topk_mask_by_extraction(score, k):
    t, n = score.shape
    col = lax.broadcasted_iota(jnp.int32, (t, n), 1).astype(F32)
    sel = jnp.zeros((t, n), F32)
    for _ in range(k):
        m = jnp.max(score, axis=-1, keepdims=True)
        first = jnp.min(jnp.where(score == m, col, float(n)), axis=-1, keepdims=True)
        pick = (col == first) & (m > -jnp.inf)
        sel = jnp.where(pick, 1.0, sel)
        score = jnp.where(col == first, -jnp.inf, score)
    return sel


def _nsa_sample_cmp_kernel(q_ref, ckv_ref, ov_ref, oc_ref, sel_ref, *, past, n_sel):
    ts = q_ref.shape[0]
    n_sel_pad = ov_ref.shape[1]
    tpos = past + lax.broadcasted_iota(jnp.int32, (ts, 1), 0)
    blk = lax.broadcasted_iota(jnp.int32, (ts, n_sel_pad), 1)
    outs, scores = [], []
    for g in range(N_KV_HEADS):
        q4 = _stack_heads(q_ref[:, g * GROUP * HEAD_DIM:(g + 1) * GROUP * HEAD_DIM]).astype(BF16)
        o_c, pb = _cmp_attend(q4, ckv_ref[2 * g].astype(BF16), ckv_ref[2 * g + 1].astype(BF16), tpos)
        outs += [o_c[r] for r in range(GROUP)]
        imp = jnp.sum(_dot(pb, ov_ref[...]).reshape(GROUP, ts, n_sel_pad), axis=0)
        scores.append(_block_scores(imp, blk, tpos, n_sel))
    oc_ref[...] = jnp.concatenate(outs, axis=1)
    sel = _topk_mask_by_extraction(jnp.concatenate(scores, axis=0), min(TOP_N, n_sel))
    for g in range(N_KV_HEADS):
        sel_ref[g] = sel[g * ts:(g + 1) * ts]


def _nsa_sample_cmp(q, ckv, row0, bs, ts, past):
    n_cmp = ckv.shape[2]
    n_sel = -(-(past + ts) // SEL_BLOCK)
    n_sel_pad = -(-n_sel // LANES) * LANES
    q_cols = N_HEADS * HEAD_DIM
    overlap = jnp.asarray(_overlap_matrix(n_cmp, n_sel_pad), BF16)
    kern = functools.partial(_nsa_sample_cmp_kernel, past=past, n_sel=n_sel)
    return pl.pallas_call(
        kern,
        grid=(bs,),
        in_specs=[pl.BlockSpec((ts, q_cols), lambda b: (row0 // ts + b, 0)),
                  pl.BlockSpec((None, KV_SLOTS, n_cmp, HEAD_DIM), lambda b: (b, 0, 0, 0)),
                  pl.BlockSpec((n_cmp, n_sel_pad), lambda b: (0, 0))],
        out_specs=[pl.BlockSpec((ts, q_cols), lambda b: (b, 0)),
                   pl.BlockSpec((None, N_KV_HEADS, ts, n_sel_pad), lambda b: (b, 0, 0, 0))],
        out_shape=[jax.ShapeDtypeStruct((bs * ts, q_cols), F32),
                   jax.ShapeDtypeStruct((bs, N_KV_HEADS, ts, n_sel_pad), F32)],
        compiler_params=_params(1),
    )(q, ckv, overlap)


def _nsa_sample_attn_kernel(pt_ref, *refs, past, n_buf, n_pages):
    del pt_ref
    page_refs = refs[:PAGES_PER_STEP]
    (q_ref, newsel_ref, selmask_ref, exp_ref, winbuf_ref, newwin_ref, oc_ref, gate_ref,
     o_ref, m_scr, l_scr, acc_scr) = refs[PAGES_PER_STEP:]
    step = pl.program_id(1)
    n_steps = n_pages // PAGES_PER_STEP
    ts = q_ref.shape[0]
    tpos = past + lax.broadcasted_iota(jnp.int32, (ts, 1), 0)
    q = q_ref[...]

    def q_heads(g):
        return _stack_heads(q[:, g * GROUP * HEAD_DIM:(g + 1) * GROUP * HEAD_DIM]).astype(BF16)

    @pl.when(step == 0)
    def _():
        m_scr[...] = jnp.full(m_scr.shape, NEG, F32)
        l_scr[...] = jnp.zeros(l_scr.shape, F32)
        acc_scr[...] = jnp.zeros(acc_scr.shape, F32)

    def sel_update(kv_refs, first_page):
        n_keys = len(kv_refs) * PAGE_SIZE
        expand = exp_ref[:, 0:n_keys]
        kpos = first_page * PAGE_SIZE + lax.broadcasted_iota(jnp.int32, (1, n_keys), 1)
        for g in range(N_KV_HEADS):
            k = jnp.concatenate([_slot_rows(r, 2 * g, PAGE_SIZE) for r in kv_refs], axis=0).astype(BF16)
            v = jnp.concatenate([_slot_rows(r, 2 * g + 1, PAGE_SIZE) for r in kv_refs], axis=0).astype(BF16)
            picked = _dot(selmask_ref[g].astype(BF16), expand)
            bias = jnp.where((picked > 0.5) & (kpos <= tpos), 0.0, NEG)
            m_scr[g], l_scr[g], acc_scr[g] = _masked_online_step(
                q_heads(g), k, v, bias, (m_scr[g], l_scr[g], acc_scr[g]))

    @pl.when(step < n_steps)
    def _():
        sel_update(page_refs, step * PAGES_PER_STEP)

    @pl.when(step == n_steps)
    def _():
        sel_update([newsel_ref], n_pages)
        gate = jax.nn.sigmoid(gate_ref[...])
        rel_buf = tpos - (past - n_buf + lax.broadcasted_iota(jnp.int32, (1, n_buf), 1))
        bias_buf = jnp.where((rel_buf >= 0) & (rel_buf < WINDOW), 0.0, NEG)
        rel_new = tpos - (past + lax.broadcasted_iota(jnp.int32, (1, PAGE_SIZE), 1))
        bias_new = jnp.where((rel_new >= 0) & (rel_new < WINDOW), 0.0, NEG)
        outs = []
        for g in range(N_KV_HEADS):
            qg = q_heads(g)
            o_s = acc_scr[g] / l_scr[g]
            carry = _masked_online_step(qg, _slot_rows(winbuf_ref, 2 * g, n_buf).astype(BF16),
                                        _slot_rows(winbuf_ref, 2 * g + 1, n_buf).astype(BF16),
                                        bias_buf, _online_init(ts))
            _, l_w, acc_w = _masked_online_step(qg, _slot_rows(newwin_ref, 2 * g, PAGE_SIZE).astype(BF16),
                                                _slot_rows(newwin_ref, 2 * g + 1, PAGE_SIZE).astype(BF16),
                                                bias_new, carry)
            o_w = acc_w / l_w
            h0 = g * GROUP * HEAD_DIM
            o_c = [oc_ref[:, h0 + r * HEAD_DIM:h0 + (r + 1) * HEAD_DIM] for r in range(GROUP)]
            outs += _gated_mix(gate, g * GROUP, o_c, o_s, o_w)
        o_ref[...] = jnp.concatenate(outs, axis=1)


def _nsa_sample_attn(q, kv_rows, gates, cache_sel, layer, n_pool, page_table, selmask, win_buf, o_c,
                     row0, bs, ts, past):
    n_pages = page_table.shape[1]
    assert n_pages % PAGES_PER_STEP == 0
    n_buf = win_buf.shape[1] // KV_SLOTS
    n_sel_pad = selmask.shape[3]
    q_cols = N_HEADS * HEAD_DIM
    rb = row0 // ts
    page_rows = PAGE_SIZE * KV_SLOTS
    bps = PAGES_PER_STEP * PAGE_SIZE // SEL_BLOCK
    assert n_sel_pad % bps == 0
    selmask = selmask.reshape(bs, N_KV_HEADS, ts, n_sel_pad // bps, bps).transpose(0, 3, 1, 2, 4)
    key_blk = np.arange(PAGES_PER_STEP * PAGE_SIZE) // SEL_BLOCK
    expand = jnp.asarray(np.arange(bps)[:, None] == key_blk[None, :], BF16)
    pad = ((0, 0), (0, (PAGE_SIZE - ts) * KV_SLOTS), (0, 0))
    new_sel = jnp.pad(kv_rows[1, row0 * KV_SLOTS:].reshape(bs, ts * KV_SLOTS, HEAD_DIM), pad)
    new_win = jnp.pad(kv_rows[2, row0 * KV_SLOTS:].reshape(bs, ts * KV_SLOTS, HEAD_DIM), pad)
    kern = functools.partial(_nsa_sample_attn_kernel, past=past, n_buf=n_buf, n_pages=n_pages)
    per_seq = lambda rows: pl.BlockSpec((None, rows, HEAD_DIM), lambda b, p, pt: (b, 0, 0))
    grid_spec = pltpu.PrefetchScalarGridSpec(
        num_scalar_prefetch=1,
        grid=(bs, n_pages // PAGES_PER_STEP + 1),
        in_specs=_page_specs(layer, n_pool, n_pages) + [
            pl.BlockSpec((ts, q_cols), lambda b, p, pt: (rb + b, 0)),
            per_seq(page_rows),
            pl.BlockSpec((None, None, N_KV_HEADS, ts, bps), lambda b, p, pt: (b, p, 0, 0, 0)),
            pl.BlockSpec(expand.shape, lambda b, p, pt: (0, 0)),
            per_seq(n_buf * KV_SLOTS),
            per_seq(page_rows),
            pl.BlockSpec((ts, q_cols), lambda b, p, pt: (b, 0)),
            pl.BlockSpec((ts, LANES), lambda b, p, pt: (rb + b, 0))],
        out_specs=pl.BlockSpec((ts, q_cols), lambda b, p, pt: (b, 0)),
        scratch_shapes=[pltpu.VMEM((N_KV_HEADS, GROUP, ts, 1), F32),
                        pltpu.VMEM((N_KV_HEADS, GROUP, ts, 1), F32),
                        pltpu.VMEM((N_KV_HEADS, GROUP, ts, HEAD_DIM), F32)],
    )
    return pl.pallas_call(
        kern,
        grid_spec=grid_spec,
        out_shape=jax.ShapeDtypeStruct((bs * ts, q_cols), F32),
        compiler_params=_params(2),
    )(page_table, *([cache_sel] * PAGES_PER_STEP), q, new_sel, selmask, expand, win_buf, new_win, o_c, gates)


def _gate_weight(w_in, layer):
    wg = w_in[layer:layer + 1, :, N_HEADS * HEAD_DIM + 3 * KV_COLS:]
    return jnp.pad(wg, ((0, 0), (0, 0), (0, LANES - wg.shape[2])))


def _attn_layer(x, layer, shapes, cache_cmp, cache_sel, n_pool, win_buf, page_table,
                norm, w_in, pool_w, pe, w1, w2, w_out):
    n_batch, seq, bs, ts, past = shapes
    m_prompt = n_batch * seq
    q_cols = N_HEADS * HEAD_DIM
    h = _rmsnorm(x, norm, BF16)
    q = _matmul(h, w_in, layer, q_cols, scale=SCALE)
    kv_rows, kv_heads = _matmul_kv(h, w_in, layer, q_cols)
    gates = _matmul(h, _gate_weight(w_in, layer), 0, LANES)

    ha, hb = _halfsums_prompt(kv_rows, pool_w, m_prompt)
    half_rows = seq // CMP_STRIDE * KV_SLOTS
    ckv_p = _cmp_mlp(ha.reshape(n_batch, half_rows, HEAD_DIM), hb.reshape(n_batch, half_rows, HEAD_DIM),
                     pool_w, pe, w1, w2)
    mix_p = _nsa_prompt(q, kv_heads, gates, ckv_p, n_batch, seq)

    sa, sb = _halfsums_paged(cache_cmp, layer, n_pool, page_table, pool_w)
    ckv_s = _cmp_mlp(sa, sb, pool_w, pe, w1, w2)
    o_c, selmask = _nsa_sample_cmp(q, ckv_s, m_prompt, bs, ts, past)
    mix_s = _nsa_sample_attn(q, kv_rows, gates, cache_sel, layer, n_pool, page_table, selmask, win_buf, o_c,
                             m_prompt, bs, ts, past)

    mixed = jnp.concatenate([mix_p, mix_s.astype(BF16)], axis=0)
    x = _matmul_residual(mixed, w_out, layer, x)

    def kv_p(branch):
        return kv_rows[branch, :m_prompt * KV_SLOTS].reshape(n_batch, seq, N_KV_HEADS, 2, HEAD_DIM)

    def kv_s(branch):
        return kv_rows[branch, m_prompt * KV_SLOTS:].reshape(bs, ts, N_KV_HEADS, 2, HEAD_DIM)

    n_buf = win_buf.shape[1] // KV_SLOTS
    win_s = jnp.concatenate([win_buf.reshape(bs, n_buf, N_KV_HEADS, 2, HEAD_DIM), kv_s(2)], axis=1)
    new = (kv_p(0), kv_p(1), kv_p(2)[:, -min(WINDOW, seq):], kv_s(0), kv_s(1), win_s[:, -n_buf:])
    return x, new


def _conv_layer(x, layer, shapes, state, norm, w_in, conv_k, w_out):
    n_batch, seq, bs, ts, _ = shapes
    m_prompt = n_batch * seq
    d = w_out.shape[1]
    h = _rmsnorm(x, norm, BF16)
    bgate, u = _matmul_conv_in(h, w_in, layer)
    k_pad = jnp.pad(conv_k, ((0, SUBLANES - CONV_W), (0, 0)))
    hist_p = jnp.zeros((n_batch, SUBLANES, d), F32)
    hist_s = jnp.pad(state, ((0, 0), (SUBLANES - (CONV_W - 1), 0), (0, 0)))
    z_p = _conv_apply(u, bgate, hist_p, k_pad, 0, n_batch, seq, BF16)
    z_s = _conv_apply(u, bgate, hist_s, k_pad, m_prompt, bs, ts, F32)
    z = jnp.concatenate([z_p, z_s.astype(BF16)], axis=0)
    x = _matmul_residual(z, w_out, layer, x)
    tail = CONV_W - 1
    new_p = jnp.stack([u[(b + 1) * seq - tail:(b + 1) * seq] for b in range(n_batch)])
    new_s = u[m_prompt:].reshape(bs, ts, d)[:, -tail:]
    return x, new_p, new_s


def _ffn(x, layer, norm, w_in, w_out):
    h = _rmsnorm(x, norm, BF16)
    hid = _matmul_swiglu(h, w_in, layer)
    return _matmul_residual(hid, w_out, layer, x)


def kernel(x_prompt, x_sample, cache_cmp_kv, cache_sel_kv, state_win_kv, state_conv, page_table, attn_norm, attn_w_in, attn_cmp_pool, attn_cmp_pe, attn_cmp_w1, attn_cmp_w2, attn_w_out, conv_norm, conv_w_in, conv_kernel, conv_w_out, ffn_norm, ffn_w_in, ffn_w_out, final_norm):
    n_batch, seq, d = x_prompt.shape
    bs, ts, _ = x_sample.shape
    past = page_table.shape[1] * PAGE_SIZE
    n_buf = state_win_kv.shape[2]
    depth = ffn_norm.shape[0]
    assert d == N_HEADS * HEAD_DIM and seq % SEL_SPAN == 0
    assert ts == SUBLANES and (n_batch * seq) % ts == 0
    shapes = (n_batch, seq, bs, ts, past)
    m_prompt = n_batch * seq
    n_attn, n_pool = cache_cmp_kv.shape[:2]
    cache_cmp = cache_cmp_kv.reshape(n_attn * n_pool, PAGE_SIZE * KV_SLOTS, HEAD_DIM)
    cache_sel = cache_sel_kv.reshape(n_attn * n_pool, PAGE_SIZE * KV_SLOTS, HEAD_DIM)

    x = jnp.concatenate([x_prompt.reshape(m_prompt, d), x_sample.reshape(bs * ts, d)], axis=0)
    attn_new, conv_new = [], []
    for i in range(depth):
        l = i // 2
        if i % 2 == 0:
            win_buf = state_win_kv[l].reshape(bs, n_buf * KV_SLOTS, HEAD_DIM)
            x, new = _attn_layer(x, l, shapes, cache_cmp, cache_sel, n_pool, win_buf, page_table,
                                 attn_norm[l], attn_w_in, attn_cmp_pool[l], attn_cmp_pe[l],
                                 attn_cmp_w1[l], attn_cmp_w2[l], attn_w_out)
            attn_new.append(new)
        else:
            x, new_p, new_s = _conv_layer(x, l, shapes, state_conv[l], conv_norm[l], conv_w_in,
                                          conv_kernel[l], conv_w_out)
            conv_new.append((new_p, new_s))
        x = _ffn(x, i, ffn_norm[i], ffn_w_in, ffn_w_out)
    y = _rmsnorm(x, final_norm, F32)
    stack = lambda k: jnp.stack([a[k] for a in attn_new])
    return (y[:m_prompt].reshape(n_batch, seq, d), y[m_prompt:].reshape(bs, ts, d),
            stack(0), stack(1), stack(2), jnp.stack([c[0] for c in conv_new]),
            stack(3), stack(4), stack(5), jnp.stack([c[1] for c in conv_new]))
```

```python
import functools

import numpy as np
import jax
import jax.numpy as jnp
from jax import lax
from jax.experimental import pallas as pl
from jax.experimental.pallas import tpu as pltpu

N_HEADS = 16
HEAD_DIM = 128
N_KV_HEADS = 4
GROUP = N_HEADS // N_KV_HEADS
KV_SLOTS = N_KV_HEADS * 2
KV_COLS = KV_SLOTS * HEAD_DIM
CMP_BLOCK = 32
CMP_STRIDE = 16
SEL_BLOCK = 64
SEL_SHIFT = 6
TOP_N = 16
N_LOCAL_FORCED = 2
WINDOW = 512
PAGE_SIZE = 128
CONV_W = 3
EPS = 1e-6
SCALE = HEAD_DIM ** -0.5
NEG = -1e30

LANES = 128
SUBLANES = 8
SEL_CHUNK = 512
SEL_SPAN = 512
Q_TILE = 64
PRE_TILE = 256
PAGES_PER_STEP = 16
VMEM_LIMIT = 56 * 1024 * 1024

F32 = jnp.float32
BF16 = jnp.bfloat16


def _params(n_axes):
    return pltpu.CompilerParams(dimension_semantics=("arbitrary",) * n_axes,
                                vmem_limit_bytes=VMEM_LIMIT)


def _pick_tile(n, limit, mult):
    best = None
    for t in range(mult, min(n, limit) + 1, mult):
        if n % t == 0:
            best = t
    assert best is not None, (n, limit, mult)
    return best


def _dot(a, b):
    return jnp.dot(a, b, preferred_element_type=F32)


def _dot_nt(a, b):
    return lax.dot_general(a, b, (((1,), (1,)), ((), ())), preferred_element_type=F32)


def _silu(x):
    return x * jax.nn.sigmoid(x)


def _slot_rows(ref, slot, n_rows):
    return ref[pl.ds(slot, n_rows, stride=KV_SLOTS), :]


def _rmsnorm_kernel(x_ref, g_ref, o_ref):
    x = x_ref[...]
    ms = jnp.mean(x * x, axis=-1, keepdims=True)
    o_ref[...] = ((x * lax.rsqrt(ms + EPS)) * g_ref[...]).astype(o_ref.dtype)


def _rmsnorm(x, gain, out_dtype):
    m, d = x.shape
    tm = _pick_tile(m, 1024, 16)
    return pl.pallas_call(
        _rmsnorm_kernel,
        grid=(m // tm,),
        in_specs=[pl.BlockSpec((tm, d), lambda i: (i, 0)),
                  pl.BlockSpec((1, d), lambda i: (0, 0))],
        out_specs=pl.BlockSpec((tm, d), lambda i: (i, 0)),
        out_shape=jax.ShapeDtypeStruct((m, d), out_dtype),
        compiler_params=_params(1),
    )(x, gain.reshape(1, d))


def _cast_weights_once(w_refs, w_scrs):
    @pl.when(pl.program_id(1) == 0)
    def _():
        for w_ref, w_scr in zip(w_refs, w_scrs):
            w_scr[...] = w_ref[...].astype(BF16)


W_BLOCK_BYTES = 16 * 1024 * 1024


def _w_spec(k, tn, layer, blk0=0):
    single = 2 * 4 * k * tn > W_BLOCK_BYTES
    return pl.BlockSpec((None, k, tn), lambda j, i: (layer, 0, blk0 + j),
                        pipeline_mode=pl.Buffered(1) if single else None)


def _wide_tile(n, k):
    return max(t for t in (LANES, 2 * LANES, 4 * LANES, 8 * LANES) if n % t == 0 and 4 * k * t <= W_BLOCK_BYTES)


def _w_scratch(k, tn, count=1):
    return [pltpu.VMEM((k, tn), BF16) for _ in range(count)]


def _mm_kernel(a_ref, w_ref, o_ref, w_scr, *, scale):
    _cast_weights_once([w_ref], [w_scr])
    acc = _dot(a_ref[...], w_scr[...])
    o_ref[...] = acc if scale is None else acc * scale


def _matmul(a, w, layer, n, scale=None):
    m, k = a.shape
    tm = _pick_tile(m, 1024, 16)
    tn = _wide_tile(n, k)
    return pl.pallas_call(
        functools.partial(_mm_kernel, scale=scale),
        grid=(n // tn, m // tm),
        in_specs=[pl.BlockSpec((tm, k), lambda j, i: (i, 0)), _w_spec(k, tn, layer)],
        out_specs=pl.BlockSpec((tm, tn), lambda j, i: (i, j)),
        out_shape=jax.ShapeDtypeStruct((m, n), F32),
        scratch_shapes=_w_scratch(k, tn),
        compiler_params=_params(2),
    )(a, w)


def _mm_kv_kernel(a_ref, w_ref, rows_ref, heads_ref, w_scr):
    _cast_weights_once([w_ref], [w_scr])
    acc = _dot(a_ref[...], w_scr[...])
    tm = acc.shape[0]
    for c in range(KV_SLOTS):
        part = acc[:, c * HEAD_DIM:(c + 1) * HEAD_DIM]
        rows_ref[pl.ds(c, tm, stride=KV_SLOTS), :] = part
        heads_ref[c] = part.astype(heads_ref.dtype)


def _matmul_kv(a, w, layer, col0):
    m, k = a.shape
    tm = _pick_tile(m, 1024, 16)
    return pl.pallas_call(
        _mm_kv_kernel,
        grid=(3, m // tm),
        in_specs=[pl.BlockSpec((tm, k), lambda j, i: (i, 0)), _w_spec(k, KV_COLS, layer, col0 // KV_COLS)],
        out_specs=[pl.BlockSpec((None, tm * KV_SLOTS, HEAD_DIM), lambda j, i: (j, i, 0)),
                   pl.BlockSpec((None, KV_SLOTS, tm, HEAD_DIM), lambda j, i: (j, 0, i, 0))],
        out_shape=[jax.ShapeDtypeStruct((3, m * KV_SLOTS, HEAD_DIM), F32),
                   jax.ShapeDtypeStruct((3, KV_SLOTS, m, HEAD_DIM), BF16)],
        scratch_shapes=_w_scratch(k, KV_COLS),
        compiler_params=_params(2),
    )(a, w)


def _mm_res_kernel(a_ref, w_ref, r_ref, o_ref, w_scr):
    _cast_weights_once([w_ref], [w_scr])
    o_ref[...] = r_ref[...] + _dot(a_ref[...], w_scr[...])


def _matmul_residual(a, w, layer, res):
    m, k = a.shape
    n = w.shape[2]
    tm = _pick_tile(m, 1024, 16)
    tn = _wide_tile(n, k)
    return pl.pallas_call(
        _mm_res_kernel,
        grid=(n // tn, m // tm),
        in_specs=[pl.BlockSpec((tm, k), lambda j, i: (i, 0)), _w_spec(k, tn, layer),
                  pl.BlockSpec((tm, tn), lambda j, i: (i, j))],
        out_specs=pl.BlockSpec((tm, tn), lambda j, i: (i, j)),
        out_shape=jax.ShapeDtypeStruct((m, n), F32),
        scratch_shapes=_w_scratch(k, tn),
        compiler_params=_params(2),
    )(a, w, res)


def _mm_res_norm_kernel(a_ref, w_ref, r_ref, g_ref, o_ref, h_ref, w_scr):
    _cast_weights_once([w_ref], [w_scr])
    y = r_ref[...] + _dot(a_ref[...], w_scr[...])
    o_ref[...] = y
    ms = jnp.mean(y * y, axis=-1, keepdims=True)
    h_ref[...] = ((y * lax.rsqrt(ms + EPS)) * g_ref[...]).astype(h_ref.dtype)


def _matmul_residual_norm(a, w, layer, res, gain):
    m, k = a.shape
    n = w.shape[2]
    assert 4 * k * n <= W_BLOCK_BYTES
    tm = _pick_tile(m, 512, 16)
    row = lambda width: pl.BlockSpec((tm, width), lambda j, i: (i, 0))
    return pl.pallas_call(
        _mm_res_norm_kernel,
        grid=(1, m // tm),
        in_specs=[row(k), _w_spec(k, n, layer), row(n), pl.BlockSpec((1, n), lambda j, i: (0, 0))],
        out_specs=[row(n), row(n)],
        out_shape=[jax.ShapeDtypeStruct((m, n), F32), jax.ShapeDtypeStruct((m, n), BF16)],
        scratch_shapes=_w_scratch(k, n),
        compiler_params=_params(2),
    )(a, w, res, gain.reshape(1, n))


def _mm_swiglu_kernel(a_ref, wg_ref, wu_ref, o_ref, wg_scr, wu_scr):
    _cast_weights_once([wg_ref, wu_ref], [wg_scr, wu_scr])
    a = a_ref[...]
    gate = _dot(a, wg_scr[...])
    up = _dot(a, wu_scr[...])
    o_ref[...] = (_silu(gate) * up).astype(o_ref.dtype)


def _matmul_swiglu(a, w, layer):
    m, k = a.shape
    f = w.shape[2] // 2
    tm = _pick_tile(m, 1536, 16)
    tn = _pick_tile(f, 512, LANES)
    nb = f // tn
    return pl.pallas_call(
        _mm_swiglu_kernel,
        grid=(nb, m // tm),
        in_specs=[pl.BlockSpec((tm, k), lambda j, i: (i, 0)), _w_spec(k, tn, layer), _w_spec(k, tn, layer, nb)],
        out_specs=pl.BlockSpec((tm, tn), lambda j, i: (i, j)),
        out_shape=jax.ShapeDtypeStruct((m, f), BF16),
        scratch_shapes=_w_scratch(k, tn, 2),
        compiler_params=_params(2),
    )(a, w, w)


def _mm_conv_in_kernel(a_ref, wb_ref, wc_ref, wx_ref, b_ref, u_ref, wb_scr, wc_scr, wx_scr):
    _cast_weights_once([wb_ref, wc_ref, wx_ref], [wb_scr, wc_scr, wx_scr])
    a = a_ref[...]
    b_ref[...] = _dot(a, wb_scr[...])
    u_ref[...] = _dot(a, wc_scr[...]) * _dot(a, wx_scr[...])


def _matmul_conv_in(a, w, layer):
    m, k = a.shape
    dc = w.shape[2] // 3
    tm = _pick_tile(m, 1024, 16)
    tn = _pick_tile(dc, 512, LANES)
    nb = dc // tn
    out = jax.ShapeDtypeStruct((m, dc), F32)
    return pl.pallas_call(
        _mm_conv_in_kernel,
        grid=(nb, m // tm),
        in_specs=[pl.BlockSpec((tm, k), lambda j, i: (i, 0)),
                  _w_spec(k, tn, layer), _w_spec(k, tn, layer, nb), _w_spec(k, tn, layer, 2 * nb)],
        out_specs=[pl.BlockSpec((tm, tn), lambda j, i: (i, j)),
                   pl.BlockSpec((tm, tn), lambda j, i: (i, j))],
        out_shape=[out, out],
        scratch_shapes=_w_scratch(k, tn, 3),
        compiler_params=_params(2),
    )(a, w, w, w)


def _conv_apply_kernel(u_ref, b_ref, prev_ref, hist_ref, k_ref, o_ref):
    i = pl.program_id(1)
    u = u_ref[...]
    rows = u.shape[0]
    tail = jnp.where(i == 0, hist_ref[...], prev_ref[...])
    row = lax.broadcasted_iota(jnp.int32, u.shape, 0)
    u1 = jnp.where(row == 0, tail[7:8], pltpu.roll(u, 1, axis=0))
    u2 = jnp.where(row == 0, tail[6:7], jnp.where(row == 1, tail[7:8], pltpu.roll(u, 2 % rows, axis=0)))
    k = k_ref[...]
    v = k[0:1] * u2 + k[1:2] * u1 + k[2:3] * u
    o_ref[...] = (b_ref[...] * v).astype(o_ref.dtype)


def _conv_apply(u, bgate, hist, conv_k, row0, n_seq, seq_len, out_dtype):
    d = u.shape[1]
    tt = _pick_tile(seq_len, 512, SUBLANES)
    tc = _pick_tile(d, 1024, LANES)
    tiles = seq_len // tt
    blk0 = row0 // tt
    sub = tt // SUBLANES

    def cur(s, i, j):
        return (blk0 + s * tiles + i, j)

    def prev(s, i, j):
        return (jnp.maximum((blk0 + s * tiles + i) * sub - 1, 0), j)

    return pl.pallas_call(
        _conv_apply_kernel,
        grid=(n_seq, tiles, d // tc),
        in_specs=[pl.BlockSpec((tt, tc), cur),
                  pl.BlockSpec((tt, tc), cur),
                  pl.BlockSpec((SUBLANES, tc), prev),
                  pl.BlockSpec((None, SUBLANES, tc), lambda s, i, j: (s, 0, j)),
                  pl.BlockSpec((SUBLANES, tc), lambda s, i, j: (0, j))],
        out_specs=pl.BlockSpec((tt, tc), lambda s, i, j: (s * tiles + i, j)),
        out_shape=jax.ShapeDtypeStruct((n_seq * seq_len, d), out_dtype),
        compiler_params=_params(3),
    )(u, bgate, u, hist, conv_k)


def _chunk_halfsums(x, wa, wb):
    n = x.shape[0] // (CMP_STRIDE * KV_SLOTS)
    x4 = x.reshape(n, CMP_STRIDE, KV_SLOTS, HEAD_DIM)
    a = jnp.sum(x4 * wa[None], axis=1).reshape(n * KV_SLOTS, HEAD_DIM)
    b = jnp.sum(x4 * wb[None], axis=1).reshape(n * KV_SLOTS, HEAD_DIM)
    return a, b


def _halfsum_kernel(x_ref, wa_ref, wb_ref, a_ref, b_ref):
    a_ref[...], b_ref[...] = _chunk_halfsums(x_ref[...], wa_ref[...], wb_ref[...])


def _halfsum_paged_kernel(pt_ref, *refs):
    del pt_ref
    page_refs = refs[:PAGES_PER_STEP]
    wa_ref, wb_ref, a_ref, b_ref = refs[PAGES_PER_STEP:]
    rows = (PAGE_SIZE // CMP_STRIDE) * KV_SLOTS
    for k, page_ref in enumerate(page_refs):
        a, b = _chunk_halfsums(page_ref[...], wa_ref[...], wb_ref[...])
        a_ref[k * rows:(k + 1) * rows, :] = a
        b_ref[k * rows:(k + 1) * rows, :] = b


def _pool_weight_maps(pool_w):
    w = pool_w.reshape(2, CMP_BLOCK // CMP_STRIDE, CMP_STRIDE)
    maps = []
    for half in range(2):
        wj = jnp.tile(w[:, half, :].T, (1, N_KV_HEADS))
        maps.append(jnp.broadcast_to(wj[:, :, None], (CMP_STRIDE, KV_SLOTS, HEAD_DIM)))
    return maps


def _halfsums_prompt(kv_rows, pool_w, n_rows):
    wa, wb = _pool_weight_maps(pool_w)
    tr = _pick_tile(n_rows, 512, LANES)
    out_rows = tr // CMP_STRIDE * KV_SLOTS
    out = jax.ShapeDtypeStruct((n_rows // CMP_STRIDE * KV_SLOTS, HEAD_DIM), F32)
    wspec = pl.BlockSpec((CMP_STRIDE, KV_SLOTS, HEAD_DIM), lambda i: (0, 0, 0))
    ospec = pl.BlockSpec((out_rows, HEAD_DIM), lambda i: (i, 0))
    return pl.pallas_call(
        _halfsum_kernel,
        grid=(n_rows // tr,),
        in_specs=[pl.BlockSpec((None, tr * KV_SLOTS, HEAD_DIM), lambda i: (0, i, 0)), wspec, wspec],
        out_specs=[ospec, ospec],
        out_shape=[out, out],
        compiler_params=_params(1),
    )(kv_rows, wa, wb)


def _page_specs(layer, n_pool, n_pages):
    def spec(k):
        def index(b, p, pt):
            page = jnp.minimum(p * PAGES_PER_STEP + k, n_pages - 1)
            return (layer * n_pool + pt[b, page], 0, 0)
        return pl.BlockSpec((None, PAGE_SIZE * KV_SLOTS, HEAD_DIM), index)
    return [spec(k) for k in range(PAGES_PER_STEP)]


def _halfsums_paged(cache, layer, n_pool, page_table, pool_w):
    bs, n_pages = page_table.shape
    assert n_pages % PAGES_PER_STEP == 0
    wa, wb = _pool_weight_maps(pool_w)
    rows = PAGES_PER_STEP * (PAGE_SIZE // CMP_STRIDE) * KV_SLOTS
    out = jax.ShapeDtypeStruct((bs, n_pages * (PAGE_SIZE // CMP_STRIDE) * KV_SLOTS, HEAD_DIM), F32)
    wspec = pl.BlockSpec((CMP_STRIDE, KV_SLOTS, HEAD_DIM), lambda b, p, pt: (0, 0, 0))
    ospec = pl.BlockSpec((None, rows, HEAD_DIM), lambda b, p, pt: (b, p, 0))
    grid_spec = pltpu.PrefetchScalarGridSpec(
        num_scalar_prefetch=1,
        grid=(bs, n_pages // PAGES_PER_STEP),
        in_specs=_page_specs(layer, n_pool, n_pages) + [wspec, wspec],
        out_specs=[ospec, ospec],
    )
    return pl.pallas_call(
        _halfsum_paged_kernel,
        grid_spec=grid_spec,
        out_shape=[out, out],
        compiler_params=_params(2),
    )(page_table, *([cache] * PAGES_PER_STEP), wa, wb)


def _cmp_mlp_kernel(a_ref, b_ref, pw_ref, pe_ref, w1_ref, w2_ref, o_ref):
    n = o_ref.shape[1]
    for k in range(2):
        pe_term = jnp.sum(pw_ref[k] * pe_ref[k], axis=0, keepdims=True)
        w1 = w1_ref[k].astype(BF16)
        w2 = w2_ref[k].astype(BF16)
        for g in range(N_KV_HEADS):
            c = 2 * g + k
            x = _slot_rows(a_ref, c, n) + pltpu.roll(_slot_rows(b_ref, c, n), n - 1, axis=0) + pe_term
            hid = _silu(_dot(x.astype(BF16), w1))
            o_ref[c] = _dot(hid.astype(BF16), w2)


def _cmp_mlp(half_a, half_b, pool_w, pe, w1, w2):
    n_seq, rows, _ = half_a.shape
    n_chunks = rows // KV_SLOTS
    spec = pl.BlockSpec((None, rows, HEAD_DIM), lambda b: (b, 0, 0))
    full3 = lambda shape: pl.BlockSpec(shape, lambda b: (0, 0, 0))
    return pl.pallas_call(
        _cmp_mlp_kernel,
        grid=(n_seq,),
        in_specs=[spec, spec, full3((2, CMP_BLOCK, 1)), full3((2, CMP_BLOCK, HEAD_DIM)),
                  full3((2, HEAD_DIM, HEAD_DIM)), full3((2, HEAD_DIM, HEAD_DIM))],
        out_specs=pl.BlockSpec((None, KV_SLOTS, n_chunks, HEAD_DIM), lambda b: (b, 0, 0, 0)),
        out_shape=jax.ShapeDtypeStruct((n_seq, KV_SLOTS, n_chunks, HEAD_DIM), F32),
        compiler_params=_params(1),
    )(half_a, half_b, pool_w.reshape(2, CMP_BLOCK, 1), pe, w1, w2)


def _overlap_matrix(n_cmp_pad, n_sel_pad):
    c_start = np.arange(n_cmp_pad)[:, None] * CMP_STRIDE
    b_start = np.arange(n_sel_pad)[None, :] * SEL_BLOCK
    return (c_start < b_start + SEL_BLOCK) & (c_start + CMP_BLOCK > b_start)


def _stack_heads(q):
    return jnp.concatenate([q[:, r * HEAD_DIM:(r + 1) * HEAD_DIM] for r in range(GROUP)], axis=0)


def _cmp_attend(q4, kc, vc, tpos):
    t = tpos.shape[0]
    n_cmp = kc.shape[0]
    s = _dot_nt(q4, kc).reshape(GROUP, t, n_cmp)
    last = lax.broadcasted_iota(jnp.int32, (1, n_cmp), 1) * CMP_STRIDE + (CMP_BLOCK - 1)
    ok = (last <= tpos)[None]
    sm = jnp.where(ok, s, NEG)
    m = jnp.max(sm, axis=-1, keepdims=True)
    e = jnp.where(ok, jnp.exp(sm - m), 0.0)
    p = e / jnp.maximum(jnp.sum(e, axis=-1, keepdims=True), 1e-30)
    pb = p.astype(BF16).reshape(GROUP * t, n_cmp)
    o_c = _dot(pb, vc).reshape(GROUP, t, HEAD_DIM)
    return o_c, pb


def _block_scores(imp, blk, tpos, n_sel):
    cur = tpos >> SEL_SHIFT
    real = blk < n_sel
    valid = (blk * SEL_BLOCK <= tpos) & real
    forced = ((blk == 0) | ((blk <= cur) & (blk > cur - N_LOCAL_FORCED))) & real
    return jnp.where(forced, jnp.inf, jnp.where(valid, imp, -jnp.inf))


def _masked_online_step(q4, k, v, bias, carry):
    m_i, l_i, acc = carry
    t, n_keys = bias.shape
    s = _dot_nt(q4, k).reshape(GROUP, t, n_keys) + bias[None]
    m_new = jnp.maximum(m_i, jnp.max(s, axis=-1, keepdims=True))
    alpha = jnp.exp(m_i - m_new)
    p = jnp.exp(s - m_new)
    l_new = alpha * l_i + jnp.sum(p, axis=-1, keepdims=True)
    pv = _dot(p.astype(BF16).reshape(GROUP * t, n_keys), v).reshape(GROUP, t, HEAD_DIM)
    return m_new, l_new, alpha * acc + pv


def _online_init(t):
    return (jnp.full((GROUP, t, 1), NEG, F32), jnp.zeros((GROUP, t, 1), F32),
            jnp.zeros((GROUP, t, HEAD_DIM), F32))


def _fold_lane_tiles(x, op):
    out = x[..., 0:LANES]
    for j in range(1, x.shape[-1] // LANES):
        out = op(out, x[..., j * LANES:(j + 1) * LANES])
    return out


def _gated_mix(gate, lane0, o_c, o_s, o_w):
    outs = []
    for r in range(GROUP):
        lanes = [lane0 + branch * N_HEADS + r for branch in range(3)]
        outs.append(gate[:, lanes[0]:lanes[0] + 1] * o_c[r]
                    + gate[:, lanes[1]:lanes[1] + 1] * o_s[r]
                    + gate[:, lanes[2]:lanes[2] + 1] * o_w[r])
    return outs


def _topk_mask_by_rank(score_t, k, n_live):
    n, lanes = score_t.shape
    tiles = [score_t[v * SUBLANES:(v + 1) * SUBLANES] for v in range(n_live // SUBLANES)]
    ranks = [jnp.zeros((SUBLANES, lanes), F32) for _ in tiles]
    row = lax.broadcasted_iota(jnp.int32, (SUBLANES, 1), 0)
    for i in range(n_live):
        si = score_t[i:i + 1, :]
        for v, tile in enumerate(tiles):
            lo = v * SUBLANES
            if lo > i:
                ahead = jnp.where(si >= tile, 1.0, 0.0)
            elif lo + SUBLANES - 1 < i:
                ahead = jnp.where(si > tile, 1.0, 0.0)
            else:
                tie = jnp.where(row + lo > i, 1.0, 0.0)
                ahead = jnp.where(si > tile, 1.0, jnp.where(si == tile, tie, 0.0))
            ranks[v] = ranks[v] + ahead
    live = [jnp.where(tile > -jnp.inf, jnp.where(rank < k, 1.0, 0.0), 0.0) for tile, rank in zip(tiles, ranks)]
    dead = [jnp.zeros((n - n_live, lanes), F32)] if n > n_live else []
    return jnp.concatenate(live + dead, axis=0)


def _head_gate_lanes(gate_ref):
    return pltpu.roll(jax.nn.sigmoid(gate_ref[...]), LANES - GROUP * pl.program_id(1), axis=1)


def _nsa_prompt_pre_kernel(q_ref, kw_ref, vw_ref, kc_ref, vc_ref, gate_ref, ov_ref, part_ref, score_ref):
    c = pl.program_id(2)
    tq = q_ref.shape[0]
    n_sel = ov_ref.shape[0]
    start = c * tq
    q4 = _stack_heads(q_ref[...]).astype(BF16)
    tpos = start + lax.broadcasted_iota(jnp.int32, (tq, 1), 0)

    o_c, pb = _cmp_attend(q4, kc_ref[...].astype(BF16), vc_ref[...].astype(BF16), tpos)

    imp_all = _dot_nt(ov_ref[...], pb)
    imp_t = imp_all[:, 0:tq]
    for r in range(1, GROUP):
        imp_t = imp_t + imp_all[:, r * tq:(r + 1) * tq]
    tpos_t = start + lax.broadcasted_iota(jnp.int32, (1, tq), 1)
    blk_t = lax.broadcasted_iota(jnp.int32, (n_sel, 1), 0)
    score_ref[...] = _block_scores(imp_t, blk_t, tpos_t, n_sel)

    win_keys = WINDOW + tq
    lo = pl.multiple_of(jnp.maximum(start - WINDOW, 0), LANES)
    rel = tpos - (lo + lax.broadcasted_iota(jnp.int32, (1, win_keys), 1))
    bias_w = jnp.where((rel >= 0) & (rel < WINDOW), 0.0, NEG)
    _, l_w, acc_w = _masked_online_step(q4, kw_ref[pl.ds(lo, win_keys), :], vw_ref[pl.ds(lo, win_keys), :],
                                        bias_w, _online_init(tq))
    o_w = acc_w / l_w

    gate = _head_gate_lanes(gate_ref)
    part_ref[...] = jnp.concatenate(
        [gate[:, r:r + 1] * o_c[r] + gate[:, 2 * N_HEADS + r:2 * N_HEADS + r + 1] * o_w[r] for r in range(GROUP)],
        axis=1)


def _nsa_prompt_sel_kernel(q_ref, ks_ref, vs_ref, score_ref, gate_ref, part_ref, exp_ref, o_ref,
                           sel_scr, os_scr, s_scr):
    c = pl.program_id(2)
    tq = Q_TILE
    seq = vs_ref.shape[0]
    n_sel = score_ref.shape[0]
    start = c * tq
    q4 = _stack_heads(q_ref[...]).astype(BF16)
    tpos = start + lax.broadcasted_iota(jnp.int32, (tq, 1), 0)
    qb_per_span = SEL_SPAN // tq

    def in_span(v):
        return (c >= v * qb_per_span) & (c < (v + 1) * qb_per_span)

    for v in range(seq // SEL_SPAN):
        @pl.when(in_span(v) & ((c & 1) == 0))
        def _():
            sel_t = _topk_mask_by_rank(score_ref[...], min(TOP_N, n_sel), (v + 1) * SEL_SPAN // SEL_BLOCK)
            sel_sq = jnp.concatenate([sel_t, jnp.zeros((LANES - n_sel, LANES), F32)], axis=0)
            sel_scr[...] = sel_sq.T

    selb = sel_scr[pl.ds(pl.multiple_of((c & 1) * tq, tq), tq), :].astype(BF16)

    for v in range(seq // SEL_SPAN):
        @pl.when(in_span(v))
        def _():
            n_keys = (v + 1) * SEL_SPAN
            m_lane = jnp.full((GROUP, tq, LANES), NEG, F32)
            for k in range(n_keys // SEL_CHUNK):
                off = k * SEL_CHUNK
                picked = _dot(selb, exp_ref[k])
                kpos = off + lax.broadcasted_iota(jnp.int32, (1, SEL_CHUNK), 1)
                bias = jnp.where(picked > 0.5, jnp.where(kpos <= tpos, 0.0, NEG), NEG)
                s_k = _dot(q4, ks_ref[:, off:off + SEL_CHUNK]).reshape(GROUP, tq, SEL_CHUNK) + bias[None]
                s_scr[k] = s_k
                m_lane = jnp.maximum(m_lane, _fold_lane_tiles(s_k, jnp.maximum))
            m_s = jnp.max(m_lane, axis=-1, keepdims=True)
            l_lane = jnp.zeros((GROUP, tq, LANES), F32)
            acc = jnp.zeros((GROUP * tq, HEAD_DIM), F32)
            for k in range(n_keys // SEL_CHUNK):
                off = k * SEL_CHUNK
                p_k = jnp.exp(s_scr[k] - m_s)
                l_lane = l_lane + _fold_lane_tiles(p_k, jnp.add)
                acc = acc + _dot(p_k.astype(BF16).reshape(GROUP * tq, SEL_CHUNK), vs_ref[off:off + SEL_CHUNK, :])
            os_scr[...] = acc.reshape(GROUP, tq, HEAD_DIM) / jnp.sum(l_lane, axis=-1, keepdims=True)

    gate = _head_gate_lanes(gate_ref)
    o_s = os_scr[...]
    mix = jnp.concatenate([gate[:, N_HEADS + r:N_HEADS + r + 1] * o_s[r] for r in range(GROUP)], axis=1)
    o_ref[...] = (part_ref[...] + mix).astype(o_ref.dtype)


def _nsa_prompt(q, kv_heads, gates, ckv, n_batch, seq):
    n_qb = seq // Q_TILE
    n_pre = seq // PRE_TILE
    n_cmp = seq // CMP_STRIDE
    n_sel = seq // SEL_BLOCK
    n_chunks = seq // SEL_CHUNK
    qw = GROUP * HEAD_DIM
    assert 2 * Q_TILE == LANES and n_sel <= LANES and n_sel % SUBLANES == 0
    assert seq % PRE_TILE == 0 and PRE_TILE % LANES == 0 and seq >= WINDOW + PRE_TILE
    overlap_t = jnp.asarray(_overlap_matrix(n_cmp, n_sel).T, BF16)
    key_blk = np.arange(seq) // SEL_BLOCK
    expand = np.arange(LANES)[:, None] == key_blk[None, :]
    expand = jnp.asarray(expand.reshape(LANES, n_chunks, SEL_CHUNK).transpose(1, 0, 2), BF16)

    def kv_spec(branch, kv):
        return pl.BlockSpec((None, None, seq, HEAD_DIM), lambda b, g, c: (branch, 2 * g + kv, b, 0))

    def ckv_spec(kv):
        return pl.BlockSpec((None, None, n_cmp, HEAD_DIM), lambda b, g, c: (b, 2 * g + kv, 0, 0))

    part, score = pl.pallas_call(
        _nsa_prompt_pre_kernel,
        grid=(n_batch, N_KV_HEADS, n_pre),
        in_specs=[pl.BlockSpec((PRE_TILE, qw), lambda b, g, c: (b * n_pre + c, g)),
                  kv_spec(2, 0), kv_spec(2, 1), ckv_spec(0), ckv_spec(1),
                  pl.BlockSpec((PRE_TILE, LANES), lambda b, g, c: (b * n_pre + c, 0)),
                  pl.BlockSpec((n_sel, n_cmp), lambda b, g, c: (0, 0))],
        out_specs=[pl.BlockSpec((PRE_TILE, qw), lambda b, g, c: (b * n_pre + c, g)),
                   pl.BlockSpec((None, None, n_sel, PRE_TILE), lambda b, g, c: (b, g, 0, c))],
        out_shape=[jax.ShapeDtypeStruct((n_batch * seq, N_HEADS * HEAD_DIM), F32),
                   jax.ShapeDtypeStruct((n_batch, N_KV_HEADS, n_sel, seq), F32)],
        compiler_params=_params(3),
    )(q, kv_heads, kv_heads, ckv, ckv, gates, overlap_t)

    k_sel_t = jnp.swapaxes(kv_heads[1, 0::2], 1, 2)
    return pl.pallas_call(
        _nsa_prompt_sel_kernel,
        grid=(n_batch, N_KV_HEADS, n_qb),
        in_specs=[pl.BlockSpec((Q_TILE, qw), lambda b, g, c: (b * n_qb + c, g)),
                  pl.BlockSpec((None, HEAD_DIM, seq), lambda b, g, c: (g, 0, b)), kv_spec(1, 1),
                  pl.BlockSpec((None, None, n_sel, LANES), lambda b, g, c: (b, g, 0, c // 2)),
                  pl.BlockSpec((Q_TILE, LANES), lambda b, g, c: (b * n_qb + c, 0)),
                  pl.BlockSpec((Q_TILE, qw), lambda b, g, c: (b * n_qb + c, g)),
                  pl.BlockSpec((n_chunks, LANES, SEL_CHUNK), lambda b, g, c: (0, 0, 0))],
        out_specs=pl.BlockSpec((Q_TILE, qw), lambda b, g, c: (b * n_qb + c, g)),
        out_shape=jax.ShapeDtypeStruct((n_batch * seq, N_HEADS * HEAD_DIM), BF16),
        scratch_shapes=[pltpu.VMEM((LANES, LANES), F32),
                        pltpu.VMEM((GROUP, Q_TILE, HEAD_DIM), F32),
                        pltpu.VMEM((n_chunks, GROUP, Q_TILE, SEL_CHUNK), F32)],
        compiler_params=_params(3),
    )(q, k_sel_t, kv_heads, score, gates, part, expand)


def _topk_mask_by_extraction(score, k):
    t, n = score.shape
    col = lax.broadcasted_iota(jnp.int32, (t, n), 1).astype(F32)
    sel = jnp.zeros((t, n), F32)
    for _ in range(k):
        m = jnp.max(score, axis=-1, keepdims=True)
        first = jnp.min(jnp.where(score == m, col, float(n)), axis=-1, keepdims=True)
        pick = (col == first) & (m > -jnp.inf)
        sel = jnp.where(pick, 1.0, sel)
        score = jnp.where(col == first, -jnp.inf, score)
    return sel


def _nsa_sample_cmp_kernel(q_ref, ckv_ref, ov_ref, oc_ref, sel_ref, *, past, n_sel):
    ts = q_ref.shape[0]
    n_sel_pad = ov_ref.shape[1]
    tpos = past + lax.broadcasted_iota(jnp.int32, (ts, 1), 0)
    blk = lax.broadcasted_iota(jnp.int32, (ts, n_sel_pad), 1)
    outs, scores = [], []
    for g in range(N_KV_HEADS):
        q4 = _stack_heads(q_ref[:, g * GROUP * HEAD_DIM:(g + 1) * GROUP * HEAD_DIM]).astype(BF16)
        o_c, pb = _cmp_attend(q4, ckv_ref[2 * g].astype(BF16), ckv_ref[2 * g + 1].astype(BF16), tpos)
        outs += [o_c[r] for r in range(GROUP)]
        imp = jnp.sum(_dot(pb, ov_ref[...]).reshape(GROUP, ts, n_sel_pad), axis=0)
        scores.append(_block_scores(imp, blk, tpos, n_sel))
    oc_ref[...] = jnp.concatenate(outs, axis=1)
    sel = _topk_mask_by_extraction(jnp.concatenate(scores, axis=0), min(TOP_N, n_sel))
    for g in range(N_KV_HEADS):
        sel_ref[g] = sel[g * ts:(g + 1) * ts]


def _nsa_sample_cmp(q, ckv, row0, bs, ts, past):
    n_cmp = ckv.shape[2]
    n_sel = -(-(past + ts) // SEL_BLOCK)
    n_sel_pad = -(-n_sel // LANES) * LANES
    q_cols = N_HEADS * HEAD_DIM
    overlap = jnp.asarray(_overlap_matrix(n_cmp, n_sel_pad), BF16)
    kern = functools.partial(_nsa_sample_cmp_kernel, past=past, n_sel=n_sel)
    return pl.pallas_call(
        kern,
        grid=(bs,),
        in_specs=[pl.BlockSpec((ts, q_cols), lambda b: (row0 // ts + b, 0)),
                  pl.BlockSpec((None, KV_SLOTS, n_cmp, HEAD_DIM), lambda b: (b, 0, 0, 0)),
                  pl.BlockSpec((n_cmp, n_sel_pad), lambda b: (0, 0))],
        out_specs=[pl.BlockSpec((ts, q_cols), lambda b: (b, 0)),
                   pl.BlockSpec((None, N_KV_HEADS, ts, n_sel_pad), lambda b: (b, 0, 0, 0))],
        out_shape=[jax.ShapeDtypeStruct((bs * ts, q_cols), F32),
                   jax.ShapeDtypeStruct((bs, N_KV_HEADS, ts, n_sel_pad), F32)],
        compiler_params=_params(1),
    )(q, ckv, overlap)


def _nsa_sample_attn_kernel(pt_ref, *refs, past, n_buf, n_pages):
    del pt_ref
    page_refs = refs[:PAGES_PER_STEP]
    (q_ref, newsel_ref, selmask_ref, exp_ref, winbuf_ref, newwin_ref, oc_ref, gate_ref,
     o_ref, m_scr, l_scr, acc_scr) = refs[PAGES_PER_STEP:]
    step = pl.program_id(1)
    n_steps = n_pages // PAGES_PER_STEP
    ts = q_ref.shape[0]
    tpos = past + lax.broadcasted_iota(jnp.int32, (ts, 1), 0)
    q = q_ref[...]

    def q_heads(g):
        return _stack_heads(q[:, g * GROUP * HEAD_DIM:(g + 1) * GROUP * HEAD_DIM]).astype(BF16)

    @pl.when(step == 0)
    def _():
        m_scr[...] = jnp.full(m_scr.shape, NEG, F32)
        l_scr[...] = jnp.zeros(l_scr.shape, F32)
        acc_scr[...] = jnp.zeros(acc_scr.shape, F32)

    def sel_update(kv_refs, first_page):
        n_keys = len(kv_refs) * PAGE_SIZE
        expand = exp_ref[:, 0:n_keys]
        kpos = first_page * PAGE_SIZE + lax.broadcasted_iota(jnp.int32, (1, n_keys), 1)
        for g in range(N_KV_HEADS):
            k = jnp.concatenate([_slot_rows(r, 2 * g, PAGE_SIZE) for r in kv_refs], axis=0).astype(BF16)
            v = jnp.concatenate([_slot_rows(r, 2 * g + 1, PAGE_SIZE) for r in kv_refs], axis=0).astype(BF16)
            picked = _dot(selmask_ref[g].astype(BF16), expand)
            bias = jnp.where((picked > 0.5) & (kpos <= tpos), 0.0, NEG)
            m_scr[g], l_scr[g], acc_scr[g] = _masked_online_step(
                q_heads(g), k, v, bias, (m_scr[g], l_scr[g], acc_scr[g]))

    @pl.when(step < n_steps)
    def _():
        sel_update(page_refs, step * PAGES_PER_STEP)

    @pl.when(step == n_steps)
    def _():
        sel_update([newsel_ref], n_pages)
        gate = jax.nn.sigmoid(gate_ref[...])
        rel_buf = tpos - (past - n_buf + lax.broadcasted_iota(jnp.int32, (1, n_buf), 1))
        bias_buf = jnp.where((rel_buf >= 0) & (rel_buf < WINDOW), 0.0, NEG)
        rel_new = tpos - (past + lax.broadcasted_iota(jnp.int32, (1, PAGE_SIZE), 1))
        bias_new = jnp.where((rel_new >= 0) & (rel_new < WINDOW), 0.0, NEG)
        outs = []
        for g in range(N_KV_HEADS):
            qg = q_heads(g)
            o_s = acc_scr[g] / l_scr[g]
            carry = _masked_online_step(qg, _slot_rows(winbuf_ref, 2 * g, n_buf).astype(BF16),
                                        _slot_rows(winbuf_ref, 2 * g + 1, n_buf).astype(BF16),
                                        bias_buf, _online_init(ts))
            _, l_w, acc_w = _masked_online_step(qg, _slot_rows(newwin_ref, 2 * g, PAGE_SIZE).astype(BF16),
                                                _slot_rows(newwin_ref, 2 * g + 1, PAGE_SIZE).astype(BF16),
                                                bias_new, carry)
            o_w = acc_w / l_w
            h0 = g * GROUP * HEAD_DIM
            o_c = [oc_ref[:, h0 + r * HEAD_DIM:h0 + (r + 1) * HEAD_DIM] for r in range(GROUP)]
            outs += _gated_mix(gate, g * GROUP, o_c, o_s, o_w)
        o_ref[...] = jnp.concatenate(outs, axis=1)


def _nsa_sample_attn(q, kv_rows, gates, cache_sel, layer, n_pool, page_table, selmask, win_buf, o_c,
                     row0, bs, ts, past):
    n_pages = page_table.shape[1]
    assert n_pages % PAGES_PER_STEP == 0
    n_buf = win_buf.shape[1] // KV_SLOTS
    n_sel_pad = selmask.shape[3]
    q_cols = N_HEADS * HEAD_DIM
    rb = row0 // ts
    page_rows = PAGE_SIZE * KV_SLOTS
    bps = PAGES_PER_STEP * PAGE_SIZE // SEL_BLOCK
    assert n_sel_pad % bps == 0
    selmask = selmask.reshape(bs, N_KV_HEADS, ts, n_sel_pad // bps, bps).transpose(0, 3, 1, 2, 4)
    key_blk = np.arange(PAGES_PER_STEP * PAGE_SIZE) // SEL_BLOCK
    expand = jnp.asarray(np.arange(bps)[:, None] == key_blk[None, :], BF16)
    pad = ((0, 0), (0, (PAGE_SIZE - ts) * KV_SLOTS), (0, 0))
    new_sel = jnp.pad(kv_rows[1, row0 * KV_SLOTS:].reshape(bs, ts * KV_SLOTS, HEAD_DIM), pad)
    new_win = jnp.pad(kv_rows[2, row0 * KV_SLOTS:].reshape(bs, ts * KV_SLOTS, HEAD_DIM), pad)
    kern = functools.partial(_nsa_sample_attn_kernel, past=past, n_buf=n_buf, n_pages=n_pages)
    per_seq = lambda rows: pl.BlockSpec((None, rows, HEAD_DIM), lambda b, p, pt: (b, 0, 0))
    grid_spec = pltpu.PrefetchScalarGridSpec(
        num_scalar_prefetch=1,
        grid=(bs, n_pages // PAGES_PER_STEP + 1),
        in_specs=_page_specs(layer, n_pool, n_pages) + [
            pl.BlockSpec((ts, q_cols), lambda b, p, pt: (rb + b, 0)),
            per_seq(page_rows),
            pl.BlockSpec((None, None, N_KV_HEADS, ts, bps), lambda b, p, pt: (b, p, 0, 0, 0)),
            pl.BlockSpec(expand.shape, lambda b, p, pt: (0, 0)),
            per_seq(n_buf * KV_SLOTS),
            per_seq(page_rows),
            pl.BlockSpec((ts, q_cols), lambda b, p, pt: (b, 0)),
            pl.BlockSpec((ts, LANES), lambda b, p, pt: (rb + b, 0))],
        out_specs=pl.BlockSpec((ts, q_cols), lambda b, p, pt: (b, 0)),
        scratch_shapes=[pltpu.VMEM((N_KV_HEADS, GROUP, ts, 1), F32),
                        pltpu.VMEM((N_KV_HEADS, GROUP, ts, 1), F32),
                        pltpu.VMEM((N_KV_HEADS, GROUP, ts, HEAD_DIM), F32)],
    )
    return pl.pallas_call(
        kern,
        grid_spec=grid_spec,
        out_shape=jax.ShapeDtypeStruct((bs * ts, q_cols), F32),
        compiler_params=_params(2),
    )(page_table, *([cache_sel] * PAGES_PER_STEP), q, new_sel, selmask, expand, win_buf, new_win, o_c, gates)


def _gate_weight(w_in, layer):
    wg = w_in[layer:layer + 1, :, N_HEADS * HEAD_DIM + 3 * KV_COLS:]
    return jnp.pad(wg, ((0, 0), (0, 0), (0, LANES - wg.shape[2])))


def _attn_layer(x, layer, shapes, cache_cmp, cache_sel, n_pool, win_buf, page_table,
                norm, w_in, pool_w, pe, w1, w2, w_out, next_norm):
    n_batch, seq, bs, ts, past = shapes
    m_prompt = n_batch * seq
    q_cols = N_HEADS * HEAD_DIM
    h = _rmsnorm(x, norm, BF16)
    q = _matmul(h, w_in, layer, q_cols, scale=SCALE)
    kv_rows, kv_heads = _matmul_kv(h, w_in, layer, q_cols)
    gates = _matmul(h, _gate_weight(w_in, layer), 0, LANES)

    ha, hb = _halfsums_prompt(kv_rows, pool_w, m_prompt)
    half_rows = seq // CMP_STRIDE * KV_SLOTS
    ckv_p = _cmp_mlp(ha.reshape(n_batch, half_rows, HEAD_DIM), hb.reshape(n_batch, half_rows, HEAD_DIM),
                     pool_w, pe, w1, w2)
    mix_p = _nsa_prompt(q, kv_heads, gates, ckv_p, n_batch, seq)

    sa, sb = _halfsums_paged(cache_cmp, layer, n_pool, page_table, pool_w)
    ckv_s = _cmp_mlp(sa, sb, pool_w, pe, w1, w2)
    o_c, selmask = _nsa_sample_cmp(q, ckv_s, m_prompt, bs, ts, past)
    mix_s = _nsa_sample_attn(q, kv_rows, gates, cache_sel, layer, n_pool, page_table, selmask, win_buf, o_c,
                             m_prompt, bs, ts, past)

    mixed = jnp.concatenate([mix_p, mix_s.astype(BF16)], axis=0)
    x, h_next = _matmul_residual_norm(mixed, w_out, layer, x, next_norm)

    def kv_p(branch):
        return kv_rows[branch, :m_prompt * KV_SLOTS].reshape(n_batch, seq, N_KV_HEADS, 2, HEAD_DIM)

    def kv_s(branch):
        return kv_rows[branch, m_prompt * KV_SLOTS:].reshape(bs, ts, N_KV_HEADS, 2, HEAD_DIM)

    n_buf = win_buf.shape[1] // KV_SLOTS
    win_s = jnp.concatenate([win_buf.reshape(bs, n_buf, N_KV_HEADS, 2, HEAD_DIM), kv_s(2)], axis=1)
    new = (kv_p(0), kv_p(1), kv_p(2)[:, -min(WINDOW, seq):], kv_s(0), kv_s(1), win_s[:, -n_buf:])
    return x, h_next, new


def _conv_layer(x, layer, shapes, state, norm, w_in, conv_k, w_out, next_norm):
    n_batch, seq, bs, ts, _ = shapes
    m_prompt = n_batch * seq
    d = w_out.shape[1]
    h = _rmsnorm(x, norm, BF16)
    bgate, u = _matmul_conv_in(h, w_in, layer)
    k_pad = jnp.pad(conv_k, ((0, SUBLANES - CONV_W), (0, 0)))
    hist_p = jnp.zeros((n_batch, SUBLANES, d), F32)
    hist_s = jnp.pad(state, ((0, 0), (SUBLANES - (CONV_W - 1), 0), (0, 0)))
    z_p = _conv_apply(u, bgate, hist_p, k_pad, 0, n_batch, seq, BF16)
    z_s = _conv_apply(u, bgate, hist_s, k_pad, m_prompt, bs, ts, F32)
    z = jnp.concatenate([z_p, z_s.astype(BF16)], axis=0)
    x, h_next = _matmul_residual_norm(z, w_out, layer, x, next_norm)
    tail = CONV_W - 1
    new_p = jnp.stack([u[(b + 1) * seq - tail:(b + 1) * seq] for b in range(n_batch)])
    new_s = u[m_prompt:].reshape(bs, ts, d)[:, -tail:]
    return x, h_next, new_p, new_s


def _ffn(x, h, layer, w_in, w_out):
    hid = _matmul_swiglu(h, w_in, layer)
    return _matmul_residual(hid, w_out, layer, x)


def kernel(x_prompt, x_sample, cache_cmp_kv, cache_sel_kv, state_win_kv, state_conv, page_table, attn_norm, attn_w_in, attn_cmp_pool, attn_cmp_pe, attn_cmp_w1, attn_cmp_w2, attn_w_out, conv_norm, conv_w_in, conv_kernel, conv_w_out, ffn_norm, ffn_w_in, ffn_w_out, final_norm):
    n_batch, seq, d = x_prompt.shape
    bs, ts, _ = x_sample.shape
    past = page_table.shape[1] * PAGE_SIZE
    n_buf = state_win_kv.shape[2]
    depth = ffn_norm.shape[0]
    assert d == N_HEADS * HEAD_DIM and seq % SEL_SPAN == 0
    assert ts == SUBLANES and (n_batch * seq) % ts == 0
    shapes = (n_batch, seq, bs, ts, past)
    m_prompt = n_batch * seq
    n_attn, n_pool = cache_cmp_kv.shape[:2]
    cache_cmp = cache_cmp_kv.reshape(n_attn * n_pool, PAGE_SIZE * KV_SLOTS, HEAD_DIM)
    cache_sel = cache_sel_kv.reshape(n_attn * n_pool, PAGE_SIZE * KV_SLOTS, HEAD_DIM)

    x = jnp.concatenate([x_prompt.reshape(m_prompt, d), x_sample.reshape(bs * ts, d)], axis=0)
    attn_new, conv_new = [], []
    for i in range(depth):
        l = i // 2
        if i % 2 == 0:
            win_buf = state_win_kv[l].reshape(bs, n_buf * KV_SLOTS, HEAD_DIM)
            x, h, new = _attn_layer(x, l, shapes, cache_cmp, cache_sel, n_pool, win_buf, page_table,
                                    attn_norm[l], attn_w_in, attn_cmp_pool[l], attn_cmp_pe[l],
                                    attn_cmp_w1[l], attn_cmp_w2[l], attn_w_out, ffn_norm[i])
            attn_new.append(new)
        else:
            x, h, new_p, new_s = _conv_layer(x, l, shapes, state_conv[l], conv_norm[l], conv_w_in,
                                             conv_kernel[l], conv_w_out, ffn_norm[i])
            conv_new.append((new_p, new_s))
        x = _ffn(x, h, i, ffn_w_in, ffn_w_out)
    y = _rmsnorm(x, final_norm, F32)
    stack = lambda k: jnp.stack([a[k] for a in attn_new])
    return (y[:m_prompt].reshape(n_batch, seq, d), y[m_prompt:].reshape(bs, ts, d),
            stack(0), stack(1), stack(2), jnp.stack([c[0] for c in conv_new]),
            stack(3), stack(4), stack(5), jnp.stack([c[1] for c in conv_new]))
```

```python
import functools

import numpy as np
import jax
import jax.numpy as jnp
from jax import lax
from jax.experimental import pallas as pl
from jax.experimental.pallas import tpu as pltpu

N_HEADS = 16
HEAD_DIM = 128
N_KV_HEADS = 4
GROUP = N_HEADS // N_KV_HEADS
KV_SLOTS = N_KV_HEADS * 2
KV_COLS = KV_SLOTS * HEAD_DIM
CMP_BLOCK = 32
CMP_STRIDE = 16
SEL_BLOCK = 64
SEL_SHIFT = 6
TOP_N = 16
N_LOCAL_FORCED = 2
WINDOW = 512
PAGE_SIZE = 128
CONV_W = 3
EPS = 1e-6
SCALE = HEAD_DIM ** -0.5
NEG = -1e30

LANES = 128
SUBLANES = 8
SEL_CHUNK = 512
SEL_SPAN = 512
Q_TILE = 64
PRE_TILE = 256
PAGES_PER_STEP = 16
VMEM_LIMIT = 56 * 1024 * 1024

F32 = jnp.float32
BF16 = jnp.bfloat16


def _params(n_axes):
    return pltpu.CompilerParams(dimension_semantics=("arbitrary",) * n_axes,
                                vmem_limit_bytes=VMEM_LIMIT)


def _pick_tile(n, limit, mult):
    best = None
    for t in range(mult, min(n, limit) + 1, mult):
        if n % t == 0:
            best = t
    assert best is not None, (n, limit, mult)
    return best


def _dot(a, b):
    return jnp.dot(a, b, preferred_element_type=F32)


def _dot_nt(a, b):
    return lax.dot_general(a, b, (((1,), (1,)), ((), ())), preferred_element_type=F32)


def _silu(x):
    return x * jax.nn.sigmoid(x)


def _slot_rows(ref, slot, n_rows):
    return ref[pl.ds(slot, n_rows, stride=KV_SLOTS), :]


def _rmsnorm_kernel(x_ref, g_ref, o_ref):
    x = x_ref[...]
    ms = jnp.mean(x * x, axis=-1, keepdims=True)
    o_ref[...] = ((x * lax.rsqrt(ms + EPS)) * g_ref[...]).astype(o_ref.dtype)


def _rmsnorm(x, gain, out_dtype, row0=0, n_rows=None):
    d = x.shape[1]
    m = x.shape[0] if n_rows is None else n_rows
    tm = _pick_tile(m, 1024, 16)
    assert row0 % tm == 0
    blk0 = row0 // tm
    return pl.pallas_call(
        _rmsnorm_kernel,
        grid=(m // tm,),
        in_specs=[pl.BlockSpec((tm, d), lambda i: (blk0 + i, 0)),
                  pl.BlockSpec((1, d), lambda i: (0, 0))],
        out_specs=pl.BlockSpec((tm, d), lambda i: (i, 0)),
        out_shape=jax.ShapeDtypeStruct((m, d), out_dtype),
        compiler_params=_params(1),
    )(x, gain.reshape(1, d))


def _cast_weights_once(w_refs, w_scrs):
    @pl.when(pl.program_id(1) == 0)
    def _():
        for w_ref, w_scr in zip(w_refs, w_scrs):
            w_scr[...] = w_ref[...].astype(BF16)


W_BLOCK_BYTES = 16 * 1024 * 1024


def _w_spec(k, tn, layer, blk0=0):
    single = 2 * 4 * k * tn > W_BLOCK_BYTES
    return pl.BlockSpec((None, k, tn), lambda j, i: (layer, 0, blk0 + j),
                        pipeline_mode=pl.Buffered(1) if single else None)


def _wide_tile(n, k):
    return max(t for t in (LANES, 2 * LANES, 4 * LANES, 8 * LANES) if n % t == 0 and 4 * k * t <= W_BLOCK_BYTES)


def _w_scratch(k, tn, count=1):
    return [pltpu.VMEM((k, tn), BF16) for _ in range(count)]


def _mm_kernel(a_ref, w_ref, o_ref, w_scr, *, scale):
    _cast_weights_once([w_ref], [w_scr])
    acc = _dot(a_ref[...], w_scr[...])
    o_ref[...] = acc if scale is None else acc * scale


def _matmul(a, w, layer, n, scale=None):
    m, k = a.shape
    tm = _pick_tile(m, 1024, 16)
    tn = _wide_tile(n, k)
    return pl.pallas_call(
        functools.partial(_mm_kernel, scale=scale),
        grid=(n // tn, m // tm),
        in_specs=[pl.BlockSpec((tm, k), lambda j, i: (i, 0)), _w_spec(k, tn, layer)],
        out_specs=pl.BlockSpec((tm, tn), lambda j, i: (i, j)),
        out_shape=jax.ShapeDtypeStruct((m, n), F32),
        scratch_shapes=_w_scratch(k, tn),
        compiler_params=_params(2),
    )(a, w)


def _mm_kv_kernel(a_ref, w_ref, rows_ref, heads_ref, w_scr):
    _cast_weights_once([w_ref], [w_scr])
    acc = _dot(a_ref[...], w_scr[...])
    tm = acc.shape[0]
    for c in range(KV_SLOTS):
        part = acc[:, c * HEAD_DIM:(c + 1) * HEAD_DIM]
        rows_ref[pl.ds(c, tm, stride=KV_SLOTS), :] = part
        heads_ref[c] = part.astype(heads_ref.dtype)


def _matmul_kv(a, w, layer, col0):
    m, k = a.shape
    tm = _pick_tile(m, 1024, 16)
    return pl.pallas_call(
        _mm_kv_kernel,
        grid=(3, m // tm),
        in_specs=[pl.BlockSpec((tm, k), lambda j, i: (i, 0)), _w_spec(k, KV_COLS, layer, col0 // KV_COLS)],
        out_specs=[pl.BlockSpec((None, tm * KV_SLOTS, HEAD_DIM), lambda j, i: (j, i, 0)),
                   pl.BlockSpec((None, KV_SLOTS, tm, HEAD_DIM), lambda j, i: (j, 0, i, 0))],
        out_shape=[jax.ShapeDtypeStruct((3, m * KV_SLOTS, HEAD_DIM), F32),
                   jax.ShapeDtypeStruct((3, KV_SLOTS, m, HEAD_DIM), BF16)],
        scratch_shapes=_w_scratch(k, KV_COLS),
        compiler_params=_params(2),
    )(a, w)


def _mm_res_kernel(a_ref, w_ref, r_ref, o_ref, w_scr):
    _cast_weights_once([w_ref], [w_scr])
    o_ref[...] = r_ref[...] + _dot(a_ref[...], w_scr[...])


def _matmul_residual(a, w, layer, res):
    m, k = a.shape
    n = w.shape[2]
    tm = _pick_tile(m, 1024, 16)
    tn = _wide_tile(n, k)
    return pl.pallas_call(
        _mm_res_kernel,
        grid=(n // tn, m // tm),
        in_specs=[pl.BlockSpec((tm, k), lambda j, i: (i, 0)), _w_spec(k, tn, layer),
                  pl.BlockSpec((tm, tn), lambda j, i: (i, j))],
        out_specs=pl.BlockSpec((tm, tn), lambda j, i: (i, j)),
        out_shape=jax.ShapeDtypeStruct((m, n), F32),
        scratch_shapes=_w_scratch(k, tn),
        compiler_params=_params(2),
    )(a, w, res)


def _mm_res_norm_kernel(a_ref, w_ref, r_ref, g_ref, o_ref, h_ref, w_scr):
    _cast_weights_once([w_ref], [w_scr])
    y = r_ref[...] + _dot(a_ref[...], w_scr[...])
    o_ref[...] = y
    ms = jnp.mean(y * y, axis=-1, keepdims=True)
    h_ref[...] = ((y * lax.rsqrt(ms + EPS)) * g_ref[...]).astype(h_ref.dtype)


def _matmul_residual_norm(a, w, layer, res, gain):
    m, k = a.shape
    n = w.shape[2]
    assert 4 * k * n <= W_BLOCK_BYTES
    tm = _pick_tile(m, 512, 16)
    row = lambda width: pl.BlockSpec((tm, width), lambda j, i: (i, 0))
    return pl.pallas_call(
        _mm_res_norm_kernel,
        grid=(1, m // tm),
        in_specs=[row(k), _w_spec(k, n, layer), row(n), pl.BlockSpec((1, n), lambda j, i: (0, 0))],
        out_specs=[row(n), row(n)],
        out_shape=[jax.ShapeDtypeStruct((m, n), F32), jax.ShapeDtypeStruct((m, n), BF16)],
        scratch_shapes=_w_scratch(k, n),
        compiler_params=_params(2),
    )(a, w, res, gain.reshape(1, n))


def _mm_swiglu_kernel(a_ref, wg_ref, wu_ref, o_ref, wg_scr, wu_scr):
    _cast_weights_once([wg_ref, wu_ref], [wg_scr, wu_scr])
    a = a_ref[...]
    gate = _dot(a, wg_scr[...])
    up = _dot(a, wu_scr[...])
    o_ref[...] = (_silu(gate) * up).astype(o_ref.dtype)


def _matmul_swiglu(a, w, layer):
    m, k = a.shape
    f = w.shape[2] // 2
    tm = _pick_tile(m, 1536, 16)
    tn = _pick_tile(f, 512, LANES)
    nb = f // tn
    return pl.pallas_call(
        _mm_swiglu_kernel,
        grid=(nb, m // tm),
        in_specs=[pl.BlockSpec((tm, k), lambda j, i: (i, 0)), _w_spec(k, tn, layer), _w_spec(k, tn, layer, nb)],
        out_specs=pl.BlockSpec((tm, tn), lambda j, i: (i, j)),
        out_shape=jax.ShapeDtypeStruct((m, f), BF16),
        scratch_shapes=_w_scratch(k, tn, 2),
        compiler_params=_params(2),
    )(a, w, w)


def _mm_conv_in_kernel(a_ref, wb_ref, wc_ref, wx_ref, b_ref, u_ref, wb_scr, wc_scr, wx_scr):
    _cast_weights_once([wb_ref, wc_ref, wx_ref], [wb_scr, wc_scr, wx_scr])
    a = a_ref[...]
    b_ref[...] = _dot(a, wb_scr[...])
    u_ref[...] = _dot(a, wc_scr[...]) * _dot(a, wx_scr[...])


def _matmul_conv_in(a, w, layer):
    m, k = a.shape
    dc = w.shape[2] // 3
    tm = _pick_tile(m, 1024, 16)
    tn = _pick_tile(dc, 512, LANES)
    nb = dc // tn
    out = jax.ShapeDtypeStruct((m, dc), F32)
    return pl.pallas_call(
        _mm_conv_in_kernel,
        grid=(nb, m // tm),
        in_specs=[pl.BlockSpec((tm, k), lambda j, i: (i, 0)),
                  _w_spec(k, tn, layer), _w_spec(k, tn, layer, nb), _w_spec(k, tn, layer, 2 * nb)],
        out_specs=[pl.BlockSpec((tm, tn), lambda j, i: (i, j)),
                   pl.BlockSpec((tm, tn), lambda j, i: (i, j))],
        out_shape=[out, out],
        scratch_shapes=_w_scratch(k, tn, 3),
        compiler_params=_params(2),
    )(a, w, w, w)


def _conv_apply_kernel(u_ref, b_ref, prev_ref, hist_ref, k_ref, o_ref):
    i = pl.program_id(1)
    u = u_ref[...]
    rows = u.shape[0]
    tail = jnp.where(i == 0, hist_ref[...], prev_ref[...])
    row = lax.broadcasted_iota(jnp.int32, u.shape, 0)
    u1 = jnp.where(row == 0, tail[7:8], pltpu.roll(u, 1, axis=0))
    u2 = jnp.where(row == 0, tail[6:7], jnp.where(row == 1, tail[7:8], pltpu.roll(u, 2 % rows, axis=0)))
    k = k_ref[...]
    v = k[0:1] * u2 + k[1:2] * u1 + k[2:3] * u
    o_ref[...] = (b_ref[...] * v).astype(o_ref.dtype)


def _conv_apply(u, bgate, hist, conv_k, row0, n_seq, seq_len, out_dtype):
    d = u.shape[1]
    tt = _pick_tile(seq_len, 512, SUBLANES)
    tc = _pick_tile(d, 1024, LANES)
    tiles = seq_len // tt
    blk0 = row0 // tt
    sub = tt // SUBLANES

    def cur(s, i, j):
        return (blk0 + s * tiles + i, j)

    def prev(s, i, j):
        return (jnp.maximum((blk0 + s * tiles + i) * sub - 1, 0), j)

    return pl.pallas_call(
        _conv_apply_kernel,
        grid=(n_seq, tiles, d // tc),
        in_specs=[pl.BlockSpec((tt, tc), cur),
                  pl.BlockSpec((tt, tc), cur),
                  pl.BlockSpec((SUBLANES, tc), prev),
                  pl.BlockSpec((None, SUBLANES, tc), lambda s, i, j: (s, 0, j)),
                  pl.BlockSpec((SUBLANES, tc), lambda s, i, j: (0, j))],
        out_specs=pl.BlockSpec((tt, tc), lambda s, i, j: (s * tiles + i, j)),
        out_shape=jax.ShapeDtypeStruct((n_seq * seq_len, d), out_dtype),
        compiler_params=_params(3),
    )(u, bgate, u, hist, conv_k)


def _chunk_halfsums(x, wa, wb):
    n = x.shape[0] // (CMP_STRIDE * KV_SLOTS)
    x4 = x.reshape(n, CMP_STRIDE, KV_SLOTS, HEAD_DIM)
    a = jnp.sum(x4 * wa[None], axis=1).reshape(n * KV_SLOTS, HEAD_DIM)
    b = jnp.sum(x4 * wb[None], axis=1).reshape(n * KV_SLOTS, HEAD_DIM)
    return a, b


def _halfsum_kernel(x_ref, wa_ref, wb_ref, a_ref, b_ref):
    a_ref[...], b_ref[...] = _chunk_halfsums(x_ref[...], wa_ref[...], wb_ref[...])


def _halfsum_paged_kernel(pt_ref, *refs):
    del pt_ref
    page_refs = refs[:PAGES_PER_STEP]
    wa_ref, wb_ref, a_ref, b_ref = refs[PAGES_PER_STEP:]
    rows = (PAGE_SIZE // CMP_STRIDE) * KV_SLOTS
    for k, page_ref in enumerate(page_refs):
        a, b = _chunk_halfsums(page_ref[...], wa_ref[...], wb_ref[...])
        a_ref[k * rows:(k + 1) * rows, :] = a
        b_ref[k * rows:(k + 1) * rows, :] = b


def _pool_weight_maps(pool_w):
    w = pool_w.reshape(2, CMP_BLOCK // CMP_STRIDE, CMP_STRIDE)
    maps = []
    for half in range(2):
        wj = jnp.tile(w[:, half, :].T, (1, N_KV_HEADS))
        maps.append(jnp.broadcast_to(wj[:, :, None], (CMP_STRIDE, KV_SLOTS, HEAD_DIM)))
    return maps


def _halfsums_prompt(kv_rows, pool_w, n_rows):
    wa, wb = _pool_weight_maps(pool_w)
    tr = _pick_tile(n_rows, 512, LANES)
    out_rows = tr // CMP_STRIDE * KV_SLOTS
    out = jax.ShapeDtypeStruct((n_rows // CMP_STRIDE * KV_SLOTS, HEAD_DIM), F32)
    wspec = pl.BlockSpec((CMP_STRIDE, KV_SLOTS, HEAD_DIM), lambda i: (0, 0, 0))
    ospec = pl.BlockSpec((out_rows, HEAD_DIM), lambda i: (i, 0))
    return pl.pallas_call(
        _halfsum_kernel,
        grid=(n_rows // tr,),
        in_specs=[pl.BlockSpec((None, tr * KV_SLOTS, HEAD_DIM), lambda i: (0, i, 0)), wspec, wspec],
        out_specs=[ospec, ospec],
        out_shape=[out, out],
        compiler_params=_params(1),
    )(kv_rows, wa, wb)


def _page_specs(layer, n_pool, n_pages):
    def spec(k):
        def index(b, p, pt):
            page = jnp.minimum(p * PAGES_PER_STEP + k, n_pages - 1)
            return (layer * n_pool + pt[b, page], 0, 0)
        return pl.BlockSpec((None, PAGE_SIZE * KV_SLOTS, HEAD_DIM), index)
    return [spec(k) for k in range(PAGES_PER_STEP)]


def _halfsums_paged(cache, layer, n_pool, page_table, pool_w):
    bs, n_pages = page_table.shape
    assert n_pages % PAGES_PER_STEP == 0
    wa, wb = _pool_weight_maps(pool_w)
    rows = PAGES_PER_STEP * (PAGE_SIZE // CMP_STRIDE) * KV_SLOTS
    out = jax.ShapeDtypeStruct((bs, n_pages * (PAGE_SIZE // CMP_STRIDE) * KV_SLOTS, HEAD_DIM), F32)
    wspec = pl.BlockSpec((CMP_STRIDE, KV_SLOTS, HEAD_DIM), lambda b, p, pt: (0, 0, 0))
    ospec = pl.BlockSpec((None, rows, HEAD_DIM), lambda b, p, pt: (b, p, 0))
    grid_spec = pltpu.PrefetchScalarGridSpec(
        num_scalar_prefetch=1,
        grid=(bs, n_pages // PAGES_PER_STEP),
        in_specs=_page_specs(layer, n_pool, n_pages) + [wspec, wspec],
        out_specs=[ospec, ospec],
    )
    return pl.pallas_call(
        _halfsum_paged_kernel,
        grid_spec=grid_spec,
        out_shape=[out, out],
        compiler_params=_params(2),
    )(page_table, *([cache] * PAGES_PER_STEP), wa, wb)


def _cmp_mlp_kernel(a_ref, b_ref, pw_ref, pe_ref, w1_ref, w2_ref, o_ref):
    n = o_ref.shape[1]
    for k in range(2):
        pe_term = jnp.sum(pw_ref[k] * pe_ref[k], axis=0, keepdims=True)
        w1 = w1_ref[k].astype(BF16)
        w2 = w2_ref[k].astype(BF16)
        for g in range(N_KV_HEADS):
            c = 2 * g + k
            x = _slot_rows(a_ref, c, n) + pltpu.roll(_slot_rows(b_ref, c, n), n - 1, axis=0) + pe_term
            hid = _silu(_dot(x.astype(BF16), w1))
            o_ref[c] = _dot(hid.astype(BF16), w2)


def _cmp_mlp(half_a, half_b, pool_w, pe, w1, w2):
    n_seq, rows, _ = half_a.shape
    n_chunks = rows // KV_SLOTS
    spec = pl.BlockSpec((None, rows, HEAD_DIM), lambda b: (b, 0, 0))
    full3 = lambda shape: pl.BlockSpec(shape, lambda b: (0, 0, 0))
    return pl.pallas_call(
        _cmp_mlp_kernel,
        grid=(n_seq,),
        in_specs=[spec, spec, full3((2, CMP_BLOCK, 1)), full3((2, CMP_BLOCK, HEAD_DIM)),
                  full3((2, HEAD_DIM, HEAD_DIM)), full3((2, HEAD_DIM, HEAD_DIM))],
        out_specs=pl.BlockSpec((None, KV_SLOTS, n_chunks, HEAD_DIM), lambda b: (b, 0, 0, 0)),
        out_shape=jax.ShapeDtypeStruct((n_seq, KV_SLOTS, n_chunks, HEAD_DIM), F32),
        compiler_params=_params(1),
    )(half_a, half_b, pool_w.reshape(2, CMP_BLOCK, 1), pe, w1, w2)


def _overlap_matrix(n_cmp_pad, n_sel_pad):
    c_start = np.arange(n_cmp_pad)[:, None] * CMP_STRIDE
    b_start = np.arange(n_sel_pad)[None, :] * SEL_BLOCK
    return (c_start < b_start + SEL_BLOCK) & (c_start + CMP_BLOCK > b_start)


def _stack_heads(q):
    return jnp.concatenate([q[:, r * HEAD_DIM:(r + 1) * HEAD_DIM] for r in range(GROUP)], axis=0)


def _cmp_attend(q4, kc, vc, tpos):
    t = tpos.shape[0]
    n_cmp = kc.shape[0]
    s = _dot_nt(q4, kc).reshape(GROUP, t, n_cmp)
    last = lax.broadcasted_iota(jnp.int32, (1, n_cmp), 1) * CMP_STRIDE + (CMP_BLOCK - 1)
    ok = (last <= tpos)[None]
    sm = jnp.where(ok, s, NEG)
    m = jnp.max(sm, axis=-1, keepdims=True)
    e = jnp.where(ok, jnp.exp(sm - m), 0.0)
    p = e / jnp.maximum(jnp.sum(e, axis=-1, keepdims=True), 1e-30)
    pb = p.astype(BF16).reshape(GROUP * t, n_cmp)
    o_c = _dot(pb, vc).reshape(GROUP, t, HEAD_DIM)
    return o_c, pb


def _block_scores(imp, blk, tpos, n_sel):
    cur = tpos >> SEL_SHIFT
    real = blk < n_sel
    valid = (blk * SEL_BLOCK <= tpos) & real
    forced = ((blk == 0) | ((blk <= cur) & (blk > cur - N_LOCAL_FORCED))) & real
    return jnp.where(forced, jnp.inf, jnp.where(valid, imp, -jnp.inf))


def _masked_online_step(q4, k, v, bias, carry):
    m_i, l_i, acc = carry
    t, n_keys = bias.shape
    s = _dot_nt(q4, k).reshape(GROUP, t, n_keys) + bias[None]
    m_new = jnp.maximum(m_i, jnp.max(s, axis=-1, keepdims=True))
    alpha = jnp.exp(m_i - m_new)
    p = jnp.exp(s - m_new)
    l_new = alpha * l_i + jnp.sum(p, axis=-1, keepdims=True)
    pv = _dot(p.astype(BF16).reshape(GROUP * t, n_keys), v).reshape(GROUP, t, HEAD_DIM)
    return m_new, l_new, alpha * acc + pv


def _online_init(t):
    return (jnp.full((GROUP, t, 1), NEG, F32), jnp.zeros((GROUP, t, 1), F32),
            jnp.zeros((GROUP, t, HEAD_DIM), F32))


def _fold_lane_tiles(x, op):
    out = x[..., 0:LANES]
    for j in range(1, x.shape[-1] // LANES):
        out = op(out, x[..., j * LANES:(j + 1) * LANES])
    return out


def _gated_mix(gate, lane0, o_c, o_s, o_w):
    outs = []
    for r in range(GROUP):
        lanes = [lane0 + branch * N_HEADS + r for branch in range(3)]
        outs.append(gate[:, lanes[0]:lanes[0] + 1] * o_c[r]
                    + gate[:, lanes[1]:lanes[1] + 1] * o_s[r]
                    + gate[:, lanes[2]:lanes[2] + 1] * o_w[r])
    return outs


def _topk_mask_by_rank(score_t, k, n_live):
    n, lanes = score_t.shape
    tiles = [score_t[v * SUBLANES:(v + 1) * SUBLANES] for v in range(n_live // SUBLANES)]
    ranks = [jnp.zeros((SUBLANES, lanes), F32) for _ in tiles]
    row = lax.broadcasted_iota(jnp.int32, (SUBLANES, 1), 0)
    for i in range(n_live):
        si = score_t[i:i + 1, :]
        for v, tile in enumerate(tiles):
            lo = v * SUBLANES
            if lo > i:
                ahead = jnp.where(si >= tile, 1.0, 0.0)
            elif lo + SUBLANES - 1 < i:
                ahead = jnp.where(si > tile, 1.0, 0.0)
            else:
                tie = jnp.where(row + lo > i, 1.0, 0.0)
                ahead = jnp.where(si > tile, 1.0, jnp.where(si == tile, tie, 0.0))
            ranks[v] = ranks[v] + ahead
    live = [jnp.where(tile > -jnp.inf, jnp.where(rank < k, 1.0, 0.0), 0.0) for tile, rank in zip(tiles, ranks)]
    dead = [jnp.zeros((n - n_live, lanes), F32)] if n > n_live else []
    return jnp.concatenate(live + dead, axis=0)


def _head_gate_lanes(gate_ref):
    return pltpu.roll(jax.nn.sigmoid(gate_ref[...]), LANES - GROUP * pl.program_id(1), axis=1)


def _nsa_prompt_pre_kernel(q_ref, kw_ref, vw_ref, kc_ref, vc_ref, gate_ref, ov_ref, part_ref, score_ref):
    c = pl.program_id(2)
    tq = q_ref.shape[0]
    n_sel = ov_ref.shape[0]
    start = c * tq
    q4 = _stack_heads(q_ref[...]).astype(BF16)
    tpos = start + lax.broadcasted_iota(jnp.int32, (tq, 1), 0)

    o_c, pb = _cmp_attend(q4, kc_ref[...].astype(BF16), vc_ref[...].astype(BF16), tpos)

    imp_all = _dot_nt(ov_ref[...], pb)
    imp_t = imp_all[:, 0:tq]
    for r in range(1, GROUP):
        imp_t = imp_t + imp_all[:, r * tq:(r + 1) * tq]
    tpos_t = start + lax.broadcasted_iota(jnp.int32, (1, tq), 1)
    blk_t = lax.broadcasted_iota(jnp.int32, (n_sel, 1), 0)
    score_ref[...] = _block_scores(imp_t, blk_t, tpos_t, n_sel)

    win_keys = WINDOW + tq
    lo = pl.multiple_of(jnp.maximum(start - WINDOW, 0), LANES)
    rel = tpos - (lo + lax.broadcasted_iota(jnp.int32, (1, win_keys), 1))
    bias_w = jnp.where((rel >= 0) & (rel < WINDOW), 0.0, NEG)
    _, l_w, acc_w = _masked_online_step(q4, kw_ref[pl.ds(lo, win_keys), :], vw_ref[pl.ds(lo, win_keys), :],
                                        bias_w, _online_init(tq))
    o_w = acc_w / l_w

    gate = _head_gate_lanes(gate_ref)
    part_ref[...] = jnp.concatenate(
        [gate[:, r:r + 1] * o_c[r] + gate[:, 2 * N_HEADS + r:2 * N_HEADS + r + 1] * o_w[r] for r in range(GROUP)],
        axis=1)


def _nsa_prompt_sel_kernel(q_ref, ks_ref, vs_ref, score_ref, gate_ref, part_ref, exp_ref, o_ref,
                           sel_scr, os_scr, s_scr):
    c = pl.program_id(2)
    tq = Q_TILE
    seq = vs_ref.shape[0]
    n_sel = score_ref.shape[0]
    start = c * tq
    q4 = _stack_heads(q_ref[...]).astype(BF16)
    tpos = start + lax.broadcasted_iota(jnp.int32, (tq, 1), 0)
    qb_per_span = SEL_SPAN // tq

    def in_span(v):
        return (c >= v * qb_per_span) & (c < (v + 1) * qb_per_span)

    for v in range(seq // SEL_SPAN):
        @pl.when(in_span(v) & ((c & 1) == 0))
        def _():
            sel_t = _topk_mask_by_rank(score_ref[...], min(TOP_N, n_sel), (v + 1) * SEL_SPAN // SEL_BLOCK)
            sel_sq = jnp.concatenate([sel_t, jnp.zeros((LANES - n_sel, LANES), F32)], axis=0)
            sel_scr[...] = sel_sq.T

    selb = sel_scr[pl.ds(pl.multiple_of((c & 1) * tq, tq), tq), :].astype(BF16)

    for v in range(seq // SEL_SPAN):
        @pl.when(in_span(v))
        def _():
            n_keys = (v + 1) * SEL_SPAN
            m_lane = jnp.full((GROUP, tq, LANES), NEG, F32)
            for k in range(n_keys // SEL_CHUNK):
                off = k * SEL_CHUNK
                picked = _dot(selb, exp_ref[k])
                kpos = off + lax.broadcasted_iota(jnp.int32, (1, SEL_CHUNK), 1)
                bias = jnp.where(picked > 0.5, jnp.where(kpos <= tpos, 0.0, NEG), NEG)
                s_k = _dot(q4, ks_ref[:, off:off + SEL_CHUNK]).reshape(GROUP, tq, SEL_CHUNK) + bias[None]
                s_scr[k] = s_k
                m_lane = jnp.maximum(m_lane, _fold_lane_tiles(s_k, jnp.maximum))
            m_s = jnp.max(m_lane, axis=-1, keepdims=True)
            l_lane = jnp.zeros((GROUP, tq, LANES), F32)
            acc = jnp.zeros((GROUP * tq, HEAD_DIM), F32)
            for k in range(n_keys // SEL_CHUNK):
                off = k * SEL_CHUNK
                p_k = jnp.exp(s_scr[k] - m_s)
                l_lane = l_lane + _fold_lane_tiles(p_k, jnp.add)
                acc = acc + _dot(p_k.astype(BF16).reshape(GROUP * tq, SEL_CHUNK), vs_ref[off:off + SEL_CHUNK, :])
            os_scr[...] = acc.reshape(GROUP, tq, HEAD_DIM) / jnp.sum(l_lane, axis=-1, keepdims=True)

    gate = _head_gate_lanes(gate_ref)
    o_s = os_scr[...]
    mix = jnp.concatenate([gate[:, N_HEADS + r:N_HEADS + r + 1] * o_s[r] for r in range(GROUP)], axis=1)
    o_ref[...] = (part_ref[...] + mix).astype(o_ref.dtype)


def _nsa_prompt(q, kv_heads, gates, ckv, n_batch, seq):
    n_qb = seq // Q_TILE
    n_pre = seq // PRE_TILE
    n_cmp = seq // CMP_STRIDE
    n_sel = seq // SEL_BLOCK
    n_chunks = seq // SEL_CHUNK
    qw = GROUP * HEAD_DIM
    assert 2 * Q_TILE == LANES and n_sel <= LANES and n_sel % SUBLANES == 0
    assert seq % PRE_TILE == 0 and PRE_TILE % LANES == 0 and seq >= WINDOW + PRE_TILE
    overlap_t = jnp.asarray(_overlap_matrix(n_cmp, n_sel).T, BF16)
    key_blk = np.arange(seq) // SEL_BLOCK
    expand = np.arange(LANES)[:, None] == key_blk[None, :]
    expand = jnp.asarray(expand.reshape(LANES, n_chunks, SEL_CHUNK).transpose(1, 0, 2), BF16)

    def kv_spec(branch, kv):
        return pl.BlockSpec((None, None, seq, HEAD_DIM), lambda b, g, c: (branch, 2 * g + kv, b, 0))

    def ckv_spec(kv):
        return pl.BlockSpec((None, None, n_cmp, HEAD_DIM), lambda b, g, c: (b, 2 * g + kv, 0, 0))

    part, score = pl.pallas_call(
        _nsa_prompt_pre_kernel,
        grid=(n_batch, N_KV_HEADS, n_pre),
        in_specs=[pl.BlockSpec((PRE_TILE, qw), lambda b, g, c: (b * n_pre + c, g)),
                  kv_spec(2, 0), kv_spec(2, 1), ckv_spec(0), ckv_spec(1),
                  pl.BlockSpec((PRE_TILE, LANES), lambda b, g, c: (b * n_pre + c, 0)),
                  pl.BlockSpec((n_sel, n_cmp), lambda b, g, c: (0, 0))],
        out_specs=[pl.BlockSpec((PRE_TILE, qw), lambda b, g, c: (b * n_pre + c, g)),
                   pl.BlockSpec((None, None, n_sel, PRE_TILE), lambda b, g, c: (b, g, 0, c))],
        out_shape=[jax.ShapeDtypeStruct((n_batch * seq, N_HEADS * HEAD_DIM), F32),
                   jax.ShapeDtypeStruct((n_batch, N_KV_HEADS, n_sel, seq), F32)],
        compiler_params=_params(3),
    )(q, kv_heads, kv_heads, ckv, ckv, gates, overlap_t)

    k_sel_t = jnp.swapaxes(kv_heads[1, 0::2], 1, 2)
    return pl.pallas_call(
        _nsa_prompt_sel_kernel,
        grid=(n_batch, N_KV_HEADS, n_qb),
        in_specs=[pl.BlockSpec((Q_TILE, qw), lambda b, g, c: (b * n_qb + c, g)),
                  pl.BlockSpec((None, HEAD_DIM, seq), lambda b, g, c: (g, 0, b)), kv_spec(1, 1),
                  pl.BlockSpec((None, None, n_sel, LANES), lambda b, g, c: (b, g, 0, c // 2)),
                  pl.BlockSpec((Q_TILE, LANES), lambda b, g, c: (b * n_qb + c, 0)),
                  pl.BlockSpec((Q_TILE, qw), lambda b, g, c: (b * n_qb + c, g)),
                  pl.BlockSpec((n_chunks, LANES, SEL_CHUNK), lambda b, g, c: (0, 0, 0))],
        out_specs=pl.BlockSpec((Q_TILE, qw), lambda b, g, c: (b * n_qb + c, g)),
        out_shape=jax.ShapeDtypeStruct((n_batch * seq, N_HEADS * HEAD_DIM), BF16),
        scratch_shapes=[pltpu.VMEM((LANES, LANES), F32),
                        pltpu.VMEM((GROUP, Q_TILE, HEAD_DIM), F32),
                        pltpu.VMEM((n_chunks, GROUP, Q_TILE, SEL_CHUNK), F32)],
        compiler_params=_params(3),
    )(q, k_sel_t, kv_heads, score, gates, part, expand)


def _topk_mask_by_extraction(score, k):
    t, n = score.shape
    col = lax.broadcasted_iota(jnp.int32, (t, n), 1).astype(F32)
    sel = jnp.zeros((t, n), F32)
    for _ in range(k):
        m = jnp.max(score, axis=-1, keepdims=True)
        first = jnp.min(jnp.where(score == m, col, float(n)), axis=-1, keepdims=True)
        pick = (col == first) & (m > -jnp.inf)
        sel = jnp.where(pick, 1.0, sel)
        score = jnp.where(col == first, -jnp.inf, score)
    return sel


def _nsa_sample_cmp_kernel(q_ref, ckv_ref, ov_ref, oc_ref, sel_ref, *, past, n_sel):
    ts = q_ref.shape[0]
    n_sel_pad = ov_ref.shape[1]
    tpos = past + lax.broadcasted_iota(jnp.int32, (ts, 1), 0)
    blk = lax.broadcasted_iota(jnp.int32, (ts, n_sel_pad), 1)
    outs, scores = [], []
    for g in range(N_KV_HEADS):
        q4 = _stack_heads(q_ref[:, g * GROUP * HEAD_DIM:(g + 1) * GROUP * HEAD_DIM]).astype(BF16)
        o_c, pb = _cmp_attend(q4, ckv_ref[2 * g].astype(BF16), ckv_ref[2 * g + 1].astype(BF16), tpos)
        outs += [o_c[r] for r in range(GROUP)]
        imp = jnp.sum(_dot(pb, ov_ref[...]).reshape(GROUP, ts, n_sel_pad), axis=0)
        scores.append(_block_scores(imp, blk, tpos, n_sel))
    oc_ref[...] = jnp.concatenate(outs, axis=1)
    sel = _topk_mask_by_extraction(jnp.concatenate(scores, axis=0), min(TOP_N, n_sel))
    for g in range(N_KV_HEADS):
        sel_ref[g] = sel[g * ts:(g + 1) * ts]


def _nsa_sample_cmp(q, ckv, row0, bs, ts, past):
    n_cmp = ckv.shape[2]
    n_sel = -(-(past + ts) // SEL_BLOCK)
    n_sel_pad = -(-n_sel // LANES) * LANES
    q_cols = N_HEADS * HEAD_DIM
    overlap = jnp.asarray(_overlap_matrix(n_cmp, n_sel_pad), BF16)
    kern = functools.partial(_nsa_sample_cmp_kernel, past=past, n_sel=n_sel)
    return pl.pallas_call(
        kern,
        grid=(bs,),
        in_specs=[pl.BlockSpec((ts, q_cols), lambda b: (row0 // ts + b, 0)),
                  pl.BlockSpec((None, KV_SLOTS, n_cmp, HEAD_DIM), lambda b: (b, 0, 0, 0)),
                  pl.BlockSpec((n_cmp, n_sel_pad), lambda b: (0, 0))],
        out_specs=[pl.BlockSpec((ts, q_cols), lambda b: (b, 0)),
                   pl.BlockSpec((None, N_KV_HEADS, ts, n_sel_pad), lambda b: (b, 0, 0, 0))],
        out_shape=[jax.ShapeDtypeStruct((bs * ts, q_cols), F32),
                   jax.ShapeDtypeStruct((bs, N_KV_HEADS, ts, n_sel_pad), F32)],
        compiler_params=_params(1),
    )(q, ckv, overlap)


def _nsa_sample_attn_kernel(pt_ref, *refs, past, n_buf, n_pages):
    del pt_ref
    page_refs = refs[:PAGES_PER_STEP]
    (q_ref, newsel_ref, selmask_ref, exp_ref, winbuf_ref, newwin_ref, oc_ref, gate_ref,
     o_ref, m_scr, l_scr, acc_scr) = refs[PAGES_PER_STEP:]
    step = pl.program_id(1)
    n_steps = n_pages // PAGES_PER_STEP
    ts = q_ref.shape[0]
    tpos = past + lax.broadcasted_iota(jnp.int32, (ts, 1), 0)
    q = q_ref[...]

    def q_heads(g):
        return _stack_heads(q[:, g * GROUP * HEAD_DIM:(g + 1) * GROUP * HEAD_DIM]).astype(BF16)

    @pl.when(step == 0)
    def _():
        m_scr[...] = jnp.full(m_scr.shape, NEG, F32)
        l_scr[...] = jnp.zeros(l_scr.shape, F32)
        acc_scr[...] = jnp.zeros(acc_scr.shape, F32)

    def sel_update(kv_refs, first_page):
        n_keys = len(kv_refs) * PAGE_SIZE
        expand = exp_ref[:, 0:n_keys]
        kpos = first_page * PAGE_SIZE + lax.broadcasted_iota(jnp.int32, (1, n_keys), 1)
        for g in range(N_KV_HEADS):
            k = jnp.concatenate([_slot_rows(r, 2 * g, PAGE_SIZE) for r in kv_refs], axis=0).astype(BF16)
            v = jnp.concatenate([_slot_rows(r, 2 * g + 1, PAGE_SIZE) for r in kv_refs], axis=0).astype(BF16)
            picked = _dot(selmask_ref[g].astype(BF16), expand)
            bias = jnp.where((picked > 0.5) & (kpos <= tpos), 0.0, NEG)
            m_scr[g], l_scr[g], acc_scr[g] = _masked_online_step(
                q_heads(g), k, v, bias, (m_scr[g], l_scr[g], acc_scr[g]))

    @pl.when(step < n_steps)
    def _():
        sel_update(page_refs, step * PAGES_PER_STEP)

    @pl.when(step == n_steps)
    def _():
        sel_update([newsel_ref], n_pages)
        gate = jax.nn.sigmoid(gate_ref[...])
        rel_buf = tpos - (past - n_buf + lax.broadcasted_iota(jnp.int32, (1, n_buf), 1))
        bias_buf = jnp.where((rel_buf >= 0) & (rel_buf < WINDOW), 0.0, NEG)
        rel_new = tpos - (past + lax.broadcasted_iota(jnp.int32, (1, PAGE_SIZE), 1))
        bias_new = jnp.where((rel_new >= 0) & (rel_new < WINDOW), 0.0, NEG)
        outs = []
        for g in range(N_KV_HEADS):
            qg = q_heads(g)
            o_s = acc_scr[g] / l_scr[g]
            carry = _masked_online_step(qg, _slot_rows(winbuf_ref, 2 * g, n_buf).astype(BF16),
                                        _slot_rows(winbuf_ref, 2 * g + 1, n_buf).astype(BF16),
                                        bias_buf, _online_init(ts))
            _, l_w, acc_w = _masked_online_step(qg, _slot_rows(newwin_ref, 2 * g, PAGE_SIZE).astype(BF16),
                                                _slot_rows(newwin_ref, 2 * g + 1, PAGE_SIZE).astype(BF16),
                                                bias_new, carry)
            o_w = acc_w / l_w
            h0 = g * GROUP * HEAD_DIM
            o_c = [oc_ref[:, h0 + r * HEAD_DIM:h0 + (r + 1) * HEAD_DIM] for r in range(GROUP)]
            outs += _gated_mix(gate, g * GROUP, o_c, o_s, o_w)
        o_ref[...] = jnp.concatenate(outs, axis=1)


def _nsa_sample_attn(q, kv_rows, gates, cache_sel, layer, n_pool, page_table, selmask, win_buf, o_c,
                     row0, bs, ts, past):
    n_pages = page_table.shape[1]
    assert n_pages % PAGES_PER_STEP == 0
    n_buf = win_buf.shape[1] // KV_SLOTS
    n_sel_pad = selmask.shape[3]
    q_cols = N_HEADS * HEAD_DIM
    rb = row0 // ts
    page_rows = PAGE_SIZE * KV_SLOTS
    bps = PAGES_PER_STEP * PAGE_SIZE // SEL_BLOCK
    assert n_sel_pad % bps == 0
    selmask = selmask.reshape(bs, N_KV_HEADS, ts, n_sel_pad // bps, bps).transpose(0, 3, 1, 2, 4)
    key_blk = np.arange(PAGES_PER_STEP * PAGE_SIZE) // SEL_BLOCK
    expand = jnp.asarray(np.arange(bps)[:, None] == key_blk[None, :], BF16)
    pad = ((0, 0), (0, (PAGE_SIZE - ts) * KV_SLOTS), (0, 0))
    new_sel = jnp.pad(kv_rows[1, row0 * KV_SLOTS:].reshape(bs, ts * KV_SLOTS, HEAD_DIM), pad)
    new_win = jnp.pad(kv_rows[2, row0 * KV_SLOTS:].reshape(bs, ts * KV_SLOTS, HEAD_DIM), pad)
    kern = functools.partial(_nsa_sample_attn_kernel, past=past, n_buf=n_buf, n_pages=n_pages)
    per_seq = lambda rows: pl.BlockSpec((None, rows, HEAD_DIM), lambda b, p, pt: (b, 0, 0))
    grid_spec = pltpu.PrefetchScalarGridSpec(
        num_scalar_prefetch=1,
        grid=(bs, n_pages // PAGES_PER_STEP + 1),
        in_specs=_page_specs(layer, n_pool, n_pages) + [
            pl.BlockSpec((ts, q_cols), lambda b, p, pt: (rb + b, 0)),
            per_seq(page_rows),
            pl.BlockSpec((None, None, N_KV_HEADS, ts, bps), lambda b, p, pt: (b, p, 0, 0, 0)),
            pl.BlockSpec(expand.shape, lambda b, p, pt: (0, 0)),
            per_seq(n_buf * KV_SLOTS),
            per_seq(page_rows),
            pl.BlockSpec((ts, q_cols), lambda b, p, pt: (b, 0)),
            pl.BlockSpec((ts, LANES), lambda b, p, pt: (rb + b, 0))],
        out_specs=pl.BlockSpec((ts, q_cols), lambda b, p, pt: (b, 0)),
        scratch_shapes=[pltpu.VMEM((N_KV_HEADS, GROUP, ts, 1), F32),
                        pltpu.VMEM((N_KV_HEADS, GROUP, ts, 1), F32),
                        pltpu.VMEM((N_KV_HEADS, GROUP, ts, HEAD_DIM), F32)],
    )
    return pl.pallas_call(
        kern,
        grid_spec=grid_spec,
        out_shape=jax.ShapeDtypeStruct((bs * ts, q_cols), F32),
        compiler_params=_params(2),
    )(page_table, *([cache_sel] * PAGES_PER_STEP), q, new_sel, selmask, expand, win_buf, new_win, o_c, gates)


def _gate_weight(w_in, layer):
    wg = w_in[layer:layer + 1, :, N_HEADS * HEAD_DIM + 3 * KV_COLS:]
    return jnp.pad(wg, ((0, 0), (0, 0), (0, LANES - wg.shape[2])))


def _attn_layer(x, layer, shapes, cache_cmp, cache_sel, n_pool, win_buf, page_table,
                norm, w_in, pool_w, pe, w1, w2, w_out, next_norm):
    n_batch, seq, bs, ts, past = shapes
    m_prompt = n_batch * seq
    q_cols = N_HEADS * HEAD_DIM
    h = _rmsnorm(x, norm, BF16)
    q = _matmul(h, w_in, layer, q_cols, scale=SCALE)
    kv_rows, kv_heads = _matmul_kv(h, w_in, layer, q_cols)
    gates = _matmul(h, _gate_weight(w_in, layer), 0, LANES)

    ha, hb = _halfsums_prompt(kv_rows, pool_w, m_prompt)
    half_rows = seq // CMP_STRIDE * KV_SLOTS
    ckv_p = _cmp_mlp(ha.reshape(n_batch, half_rows, HEAD_DIM), hb.reshape(n_batch, half_rows, HEAD_DIM),
                     pool_w, pe, w1, w2)
    mix_p = _nsa_prompt(q, kv_heads, gates, ckv_p, n_batch, seq)

    sa, sb = _halfsums_paged(cache_cmp, layer, n_pool, page_table, pool_w)
    ckv_s = _cmp_mlp(sa, sb, pool_w, pe, w1, w2)
    o_c, selmask = _nsa_sample_cmp(q, ckv_s, m_prompt, bs, ts, past)
    mix_s = _nsa_sample_attn(q, kv_rows, gates, cache_sel, layer, n_pool, page_table, selmask, win_buf, o_c,
                             m_prompt, bs, ts, past)

    mixed = jnp.concatenate([mix_p, mix_s.astype(BF16)], axis=0)
    x, h_next = _matmul_residual_norm(mixed, w_out, layer, x, next_norm)

    def kv_p(branch):
        return kv_rows[branch, :m_prompt * KV_SLOTS].reshape(n_batch, seq, N_KV_HEADS, 2, HEAD_DIM)

    def kv_s(branch):
        return kv_rows[branch, m_prompt * KV_SLOTS:].reshape(bs, ts, N_KV_HEADS, 2, HEAD_DIM)

    n_buf = win_buf.shape[1] // KV_SLOTS
    win_s = jnp.concatenate([win_buf.reshape(bs, n_buf, N_KV_HEADS, 2, HEAD_DIM), kv_s(2)], axis=1)
    new = (kv_p(0), kv_p(1), kv_p(2)[:, -min(WINDOW, seq):], kv_s(0), kv_s(1), win_s[:, -n_buf:])
    return x, h_next, new


def _conv_layer(x, layer, shapes, state, norm, w_in, conv_k, w_out, next_norm):
    n_batch, seq, bs, ts, _ = shapes
    m_prompt = n_batch * seq
    d = w_out.shape[1]
    h = _rmsnorm(x, norm, BF16)
    bgate, u = _matmul_conv_in(h, w_in, layer)
    k_pad = jnp.pad(conv_k, ((0, SUBLANES - CONV_W), (0, 0)))
    hist_p = jnp.zeros((n_batch, SUBLANES, d), F32)
    hist_s = jnp.pad(state, ((0, 0), (SUBLANES - (CONV_W - 1), 0), (0, 0)))
    z_p = _conv_apply(u, bgate, hist_p, k_pad, 0, n_batch, seq, BF16)
    z_s = _conv_apply(u, bgate, hist_s, k_pad, m_prompt, bs, ts, F32)
    z = jnp.concatenate([z_p, z_s.astype(BF16)], axis=0)
    x, h_next = _matmul_residual_norm(z, w_out, layer, x, next_norm)
    tail = CONV_W - 1
    new_p = jnp.stack([u[(b + 1) * seq - tail:(b + 1) * seq] for b in range(n_batch)])
    new_s = u[m_prompt:].reshape(bs, ts, d)[:, -tail:]
    return x, h_next, new_p, new_s


def _ffn(x, h, layer, w_in, w_out):
    hid = _matmul_swiglu(h, w_in, layer)
    return _matmul_residual(hid, w_out, layer, x)


def kernel(x_prompt, x_sample, cache_cmp_kv, cache_sel_kv, state_win_kv, state_conv, page_table, attn_norm, attn_w_in, attn_cmp_pool, attn_cmp_pe, attn_cmp_w1, attn_cmp_w2, attn_w_out, conv_norm, conv_w_in, conv_kernel, conv_w_out, ffn_norm, ffn_w_in, ffn_w_out, final_norm):
    n_batch, seq, d = x_prompt.shape
    bs, ts, _ = x_sample.shape
    past = page_table.shape[1] * PAGE_SIZE
    n_buf = state_win_kv.shape[2]
    depth = ffn_norm.shape[0]
    assert d == N_HEADS * HEAD_DIM and seq % SEL_SPAN == 0
    assert ts == SUBLANES and (n_batch * seq) % ts == 0
    shapes = (n_batch, seq, bs, ts, past)
    m_prompt = n_batch * seq
    n_attn, n_pool = cache_cmp_kv.shape[:2]
    cache_cmp = cache_cmp_kv.reshape(n_attn * n_pool, PAGE_SIZE * KV_SLOTS, HEAD_DIM)
    cache_sel = cache_sel_kv.reshape(n_attn * n_pool, PAGE_SIZE * KV_SLOTS, HEAD_DIM)

    x = jnp.concatenate([x_prompt.reshape(m_prompt, d), x_sample.reshape(bs * ts, d)], axis=0)
    attn_new, conv_new = [], []
    for i in range(depth):
        l = i // 2
        if i % 2 == 0:
            win_buf = state_win_kv[l].reshape(bs, n_buf * KV_SLOTS, HEAD_DIM)
            x, h, new = _attn_layer(x, l, shapes, cache_cmp, cache_sel, n_pool, win_buf, page_table,
                                    attn_norm[l], attn_w_in, attn_cmp_pool[l], attn_cmp_pe[l],
                                    attn_cmp_w1[l], attn_cmp_w2[l], attn_w_out, ffn_norm[i])
            attn_new.append(new)
        else:
            x, h, new_p, new_s = _conv_layer(x, l, shapes, state_conv[l], conv_norm[l], conv_w_in,
                                             conv_kernel[l], conv_w_out, ffn_norm[i])
            conv_new.append((new_p, new_s))
        x = _ffn(x, h, i, ffn_w_in, ffn_w_out)
    y_prompt = _rmsnorm(x, final_norm, F32, 0, m_prompt)
    y_sample = _rmsnorm(x, final_norm, F32, m_prompt, bs * ts)
    stack = lambda k: jnp.stack([a[k] for a in attn_new])
    return (y_prompt.reshape(n_batch, seq, d), y_sample.reshape(bs, ts, d),
            stack(0), stack(1), stack(2), jnp.stack([c[0] for c in conv_new]),
            stack(3), stack(4), stack(5), jnp.stack([c[1] for c in conv_new]))
```
